```python
import jax, jax.numpy as jnp
from jax import lax
import numpy as np

D_MODEL = 1024
BATCH = 8
SEQ = 2048
DEPTH = 4
DEC_BATCH = 128
DEC_SEQ = 8
PAST_LEN = 16384
PAGE_SIZE = 128

N_MIXERS = 2
N_GDN = (DEPTH + 1) // 2
N_HGRN = DEPTH // 2
GDN_HEADS = 8
GDN_DK = D_MODEL // GDN_HEADS
GDN_DV = D_MODEL // GDN_HEADS
GDN_CONV = 4
GDN_CHUNK = 64
HGRN_HEADS = 8
HGRN_DEXP = D_MODEL // HGRN_HEADS
HGRN_DV = D_MODEL // HGRN_HEADS
HGRN_CHUNK = 16
D_FF = 2816
FFN_CONV = 3
N_MOD = 6
EPS = 1e-6
LB_FLOOR = 1e-30

GDN_QK = GDN_HEADS * GDN_DK
GDN_V = GDN_HEADS * GDN_DV
GDN_IN = 2 * GDN_QK + 2 * GDN_V + 2 * GDN_HEADS
HGRN_QF = HGRN_HEADS * HGRN_DEXP
HGRN_V = HGRN_HEADS * HGRN_DV
HGRN_IN = 2 * HGRN_QF + 2 * HGRN_V

kernel_name = "hybrid_gdn_hgrn2_convffn_adaln_step"

F32 = jnp.float32


def _rmsnorm(x, g):
    xf = x.astype(F32)
    y = xf * lax.rsqrt(jnp.mean(xf * xf, axis=-1, keepdims=True) + EPS)
    return (y * g.astype(F32)).astype(x.dtype)


def _l2norm(x):
    xf = x.astype(F32)
    return xf * lax.rsqrt(jnp.sum(xf * xf, axis=-1, keepdims=True) + EPS)


def _masked_exp(mask, d):
    return jnp.where(mask, jnp.exp(jnp.where(mask, d, 0.0)), 0.0)


def _causal_dwconv(x, buf, w, b):
    width = w.shape[0]
    L = x.shape[1]
    xp = jnp.concatenate([buf.astype(x.dtype), x], axis=1)
    y = xp[:, 0:L] * w[0]
    for j in range(1, width):
        y = y + xp[:, j:j + L] * w[j]
    return y + b, xp[:, xp.shape[1] - (width - 1):]


def _chunk(t, C):
    B, L = t.shape[:2]
    n = -(-L // C)
    t = jnp.pad(t, [(0, 0), (0, n * C - L)] + [(0, 0)] * (t.ndim - 2))
    t = t.reshape((B, n, C) + t.shape[2:])
    return t.transpose((1, 0, 3, 2) + tuple(range(4, t.ndim)))


def _unchunk(o, L):
    n, B, H, C, V = o.shape
    return o.transpose(1, 0, 3, 2, 4).reshape(B, n * C, H, V)[:, :L]


def _gated_delta(q, k, v, beta, g, S0):
    L = q.shape[1]
    C = min(GDN_CHUNK, L)
    qc, kc, vc = [_chunk(t.astype(F32), C) for t in (q, k, v)]
    bc = _chunk(beta.astype(F32), C)
    G = jnp.cumsum(_chunk(g.astype(F32), C), axis=-1)
    incl = jnp.tril(jnp.ones((C, C), bool))
    strict = jnp.tril(jnp.ones((C, C), bool), -1)
    decay = _masked_exp(incl, G[..., :, None] - G[..., None, :])
    kb = kc * bc[..., None]
    A = jnp.where(strict, jnp.einsum('nbhik,nbhjk->nbhij', kb, kc) * decay, 0.0)
    eye = jnp.eye(C, dtype=F32)
    rhs = jnp.concatenate([vc * bc[..., None], kb * jnp.exp(G)[..., None]], axis=-1)
    sol = lax.linalg.triangular_solve(A + eye, rhs, left_side=True, lower=True, unit_diagonal=True)
    dv = vc.shape[-1]
    uv, w = sol[..., :dv], sol[..., dv:]
    attn = jnp.einsum('nbhik,nbhjk->nbhij', qc, kc) * decay
    qg = qc * jnp.exp(G)[..., None]
    GL = G[..., -1]
    kd = kc * jnp.exp(GL[..., None] - G)[..., None]

    def step(S, xs):
        qg_, uv_, w_, kd_, gl_, attn_ = xs
        u = uv_ - jnp.einsum('bhck,bhkv->bhcv', w_, S)
        o = jnp.einsum('bhck,bhkv->bhcv', qg_, S) + jnp.einsum('bhij,bhjv->bhiv', attn_, u)
        S = S * gl_[..., None, None] + jnp.einsum('bhck,bhcv->bhkv', kd_, u)
        return S, o

    S, o = lax.scan(step, S0.astype(F32), (qg, uv, w, kd, jnp.exp(GL), attn))
    return _unchunk(o, L), S.astype(S0.dtype)


def _hgrn_chunked(q, k, logf, v, S0):
    L = q.shape[1]
    C = min(HGRN_CHUNK, L)
    qc, kc, lc, vc = [_chunk(t.astype(F32), C) for t in (q, k, logf, v)]
    incl = jnp.tril(jnp.ones((C, C), bool))[:, :, None]

    def step(S, xs):
        qb, kb, lb, vb = xs
        G = jnp.cumsum(lb, axis=2)
        dec = _masked_exp(incl, G[:, :, :, None, :] - G[:, :, None, :, :])
        attn = jnp.einsum('bhik,bhjk,bhijk->bhij', qb, kb, dec)
        o = jnp.einsum('bhik,bhkv->bhiv', qb * jnp.exp(G), S) + jnp.einsum('bhij,bhjv->bhiv', attn, vb)
        GL = G[:, :, -1:, :]
        S = jnp.exp(GL[:, :, 0, :])[..., None] * S + jnp.einsum('bhjk,bhjv->bhkv', kb * jnp.exp(GL - G), vb)
        return S, o

    S, o = lax.scan(step, S0.astype(F32), (qc, kc, lc, vc))
    return _unchunk(o, L), S.astype(S0.dtype)


def _gdn_mixer(h, S0, buf, w_in, conv_w, conv_b, a_log, dt_bias, norm_g, w_out):
    B, L, _ = h.shape
    proj = h @ w_in
    c1 = 2 * GDN_QK + GDN_V
    qkv, gt, br, ar = jnp.split(proj, [c1, c1 + GDN_V, c1 + GDN_V + GDN_HEADS], axis=-1)
    qkv, new_buf = _causal_dwconv(qkv, buf, conv_w, conv_b)
    qkv = jax.nn.silu(qkv)
    q, k, v = jnp.split(qkv, [GDN_QK, 2 * GDN_QK], axis=-1)
    shp = (B, L, GDN_HEADS, -1)
    q = _l2norm(q.reshape(shp)) * (GDN_DK ** -0.5)
    k = _l2norm(k.reshape(shp))
    beta = jax.nn.sigmoid(br.astype(F32))
    g = -jnp.exp(a_log.astype(F32)) * jax.nn.softplus(ar.astype(F32) + dt_bias.astype(F32))
    o, S = _gated_delta(q, k, v.reshape(shp), beta, g, S0)
    o = _rmsnorm(o.astype(h.dtype), norm_g) * jax.nn.silu(gt.reshape(shp))
    return o.reshape(B, L, GDN_V) @ w_out, S, new_buf


def _hgrn_mixer(h, S0, lb, w_in, norm_g, w_out):
    B, L, _ = h.shape
    proj = h @ w_in
    q, fr, i, gt = jnp.split(proj, [HGRN_QF, 2 * HGRN_QF, 2 * HGRN_QF + HGRN_V], axis=-1)
    fr = fr.astype(F32)
    lb = lb.astype(F32)
    logf = jnp.logaddexp(jnp.log(jnp.maximum(lb, LB_FLOOR)), jnp.log1p(-lb) + jax.nn.log_sigmoid(fr))
    k = (1.0 - lb) * jax.nn.sigmoid(-fr)
    shp = (B, L, HGRN_HEADS, -1)
    o, S = _hgrn_chunked(jax.nn.silu(q).reshape(shp), k.reshape(shp), logf.reshape(shp), i.reshape(shp), S0)
    o = _rmsnorm(o.astype(h.dtype), norm_g) * jax.nn.silu(gt.reshape(shp))
    return o.reshape(B, L, HGRN_V) @ w_out, S


def _conv_ffn(h, buf, w_gu, conv_w, conv_b, w_down):
    gt, up = jnp.split(h @ w_gu, 2, axis=-1)
    gt, new_buf = _causal_dwconv(gt, buf, conv_w, conv_b)
    return (jax.nn.silu(gt) * up) @ w_down, new_buf


def _trunk(x, c, s_gdn, s_gconv, s_hgrn, s_fconv,
           ada_w, ada_b, norm_pre_mix, norm_post_mix, norm_pre_ffn, norm_post_ffn,
           gdn_w_in, gdn_conv_w, gdn_conv_b, gdn_a_log, gdn_dt_bias, gdn_norm, gdn_w_out,
           hgrn_lb, hgrn_w_in, hgrn_norm, hgrn_w_out,
           ffn_w_gu, ffn_conv_w, ffn_conv_b, ffn_w_down):
    p = jax.nn.softmax(hgrn_lb.astype(F32), axis=0)
    lower = jnp.cumsum(p, axis=0) - p[0]
    cs = jax.nn.silu(c)
    new_gdn, new_gconv, new_hgrn, new_fconv = [], [], [], []
    for layer in range(DEPTH):
        mod = (cs @ ada_w[layer] + ada_b[layer])[:, None, :]
        sh1, sc1, g1, sh2, sc2, g2 = jnp.split(mod, N_MOD, axis=-1)
        h = _rmsnorm(x, norm_pre_mix[layer]) * (1 + sc1) + sh1
        j = layer // N_MIXERS
        if layer % N_MIXERS == 0:
            out, S, buf = _gdn_mixer(h, s_gdn[j], s_gconv[j], gdn_w_in[j], gdn_conv_w[j], gdn_conv_b[j],
                                     gdn_a_log[j], gdn_dt_bias[j], gdn_norm[j], gdn_w_out[j])
            new_gdn.append(S)
            new_gconv.append(buf)
        else:
            out, S = _hgrn_mixer(h, s_hgrn[j], lower[j], hgrn_w_in[j], hgrn_norm[j], hgrn_w_out[j])
            new_hgrn.append(S)
        x = x + (1 + g1) * _rmsnorm(out, norm_post_mix[layer])
        h = _rmsnorm(x, norm_pre_ffn[layer]) * (1 + sc2) + sh2
        out, fbuf = _conv_ffn(h, s_fconv[layer], ffn_w_gu[layer], ffn_conv_w[layer], ffn_conv_b[layer],
                              ffn_w_down[layer])
        new_fconv.append(fbuf)
        x = x + (1 + g2) * _rmsnorm(out, norm_post_ffn[layer])
    return x, jnp.stack(new_gdn), jnp.stack(new_gconv), jnp.stack(new_hgrn), jnp.stack(new_fconv)


def setup_inputs(seed: int = 0) -> dict:
    key = jax.random.key(seed)
    ks = iter(jax.random.split(key, 40))

    def nrm(shape, s):
        return jax.random.normal(next(ks), shape, F32) * s

    cqkv = 2 * GDN_QK + GDN_V
    dt = jnp.exp(jax.random.uniform(next(ks), (N_GDN, GDN_HEADS), F32, float(np.log(1e-3)), float(np.log(1e-1))))
    return {
        "x_prompt": nrm((BATCH, SEQ, D_MODEL), 1.0),
        "x_sample": nrm((DEC_BATCH, DEC_SEQ, D_MODEL), 1.0),
        "state_gdn": nrm((N_GDN, DEC_BATCH, GDN_HEADS, GDN_DK, GDN_DV), 0.1),
        "state_gdn_conv": nrm((N_GDN, DEC_BATCH, GDN_CONV - 1, cqkv), 1.0),
        "state_hgrn": nrm((N_HGRN, DEC_BATCH, HGRN_HEADS, HGRN_DEXP, HGRN_DV), 0.3),
        "state_ffn_conv": nrm((DEPTH, DEC_BATCH, FFN_CONV - 1, D_FF), 1.0),
        "c_prompt": nrm((BATCH, D_MODEL), 1.0),
        "c_sample": nrm((DEC_BATCH, D_MODEL), 1.0),
        "ada_w": nrm((DEPTH, D_MODEL, N_MOD * D_MODEL), 0.1 * D_MODEL ** -0.5),
        "ada_b": nrm((DEPTH, N_MOD * D_MODEL), 0.01),
        "norm_pre_mix": 1.0 + nrm((DEPTH, D_MODEL), 0.05),
        "norm_post_mix": 1.0 + nrm((DEPTH, D_MODEL), 0.05),
        "norm_pre_ffn": 1.0 + nrm((DEPTH, D_MODEL), 0.05),
        "norm_post_ffn": 1.0 + nrm((DEPTH, D_MODEL), 0.05),
        "gdn_w_in": nrm((N_GDN, D_MODEL, GDN_IN), D_MODEL ** -0.5),
        "gdn_conv_w": nrm((N_GDN, GDN_CONV, cqkv), GDN_CONV ** -0.5),
        "gdn_conv_b": nrm((N_GDN, cqkv), 0.01),
        "gdn_a_log": jnp.log(jax.random.uniform(next(ks), (N_GDN, GDN_HEADS), F32, 1.0, 16.0)),
        "gdn_dt_bias": dt + jnp.log(-jnp.expm1(-dt)),
        "gdn_norm": 1.0 + nrm((N_GDN, GDN_DV), 0.05),
        "gdn_w_out": nrm((N_GDN, GDN_V, D_MODEL), GDN_V ** -0.5),
        "hgrn_lb": nrm((N_HGRN, HGRN_QF), 1.0),
        "hgrn_w_in": nrm((N_HGRN, D_MODEL, HGRN_IN), D_MODEL ** -0.5),
        "hgrn_norm": 1.0 + nrm((N_HGRN, HGRN_DV), 0.05),
        "hgrn_w_out": nrm((N_HGRN, HGRN_V, D_MODEL), HGRN_V ** -0.5),
        "ffn_w_gu": nrm((DEPTH, D_MODEL, 2 * D_FF), D_MODEL ** -0.5),
        "ffn_conv_w": nrm((DEPTH, FFN_CONV, D_FF), FFN_CONV ** -0.5),
        "ffn_conv_b": nrm((DEPTH, D_FF), 0.01),
        "ffn_w_down": nrm((DEPTH, D_FF, D_MODEL), D_FF ** -0.5),
    }


def reference(x_prompt, x_sample, state_gdn, state_gdn_conv, state_hgrn, state_ffn_conv, c_prompt, c_sample,
              ada_w, ada_b, norm_pre_mix, norm_post_mix, norm_pre_ffn, norm_post_ffn,
              gdn_w_in, gdn_conv_w, gdn_conv_b, gdn_a_log, gdn_dt_bias, gdn_norm, gdn_w_out,
              hgrn_lb, hgrn_w_in, hgrn_norm, hgrn_w_out,
              ffn_w_gu, ffn_conv_w, ffn_conv_b, ffn_w_down):
    weights = (ada_w, ada_b, norm_pre_mix, norm_post_mix, norm_pre_ffn, norm_post_ffn,
               gdn_w_in, gdn_conv_w, gdn_conv_b, gdn_a_log, gdn_dt_bias, gdn_norm, gdn_w_out,
               hgrn_lb, hgrn_w_in, hgrn_norm, hgrn_w_out,
               ffn_w_gu, ffn_conv_w, ffn_conv_b, ffn_w_down)
    bp = x_prompt.shape[0]
    dt = x_prompt.dtype
    z_gdn = jnp.zeros((N_GDN, bp) + state_gdn.shape[2:], dt)
    z_gconv = jnp.zeros((N_GDN, bp) + state_gdn_conv.shape[2:], dt)
    z_hgrn = jnp.zeros((N_HGRN, bp) + state_hgrn.shape[2:], dt)
    z_fconv = jnp.zeros((DEPTH, bp) + state_ffn_conv.shape[2:], dt)
    y_prompt, p_gdn, p_gconv, p_hgrn, p_fconv = _trunk(x_prompt, c_prompt, z_gdn, z_gconv, z_hgrn, z_fconv, *weights)
    y_sample, s_gdn, s_gconv, s_hgrn, s_fconv = _trunk(x_sample, c_sample, state_gdn, state_gdn_conv, state_hgrn,
                                                       state_ffn_conv, *weights)
    return (y_prompt, y_sample, p_gdn, p_gconv, p_hgrn, p_fconv, s_gdn, s_gconv, s_hgrn, s_fconv)
```

```python
import functools

import numpy as np
import jax
import jax.numpy as jnp
from jax import lax
from jax.experimental import pallas as pl
from jax.experimental.pallas import tpu as pltpu

F32 = jnp.float32
BF16 = jnp.bfloat16
EPS = 1e-6
LB_FLOOR = 1e-30
N_MOD = 6
N_MIXERS = 2
HEADS = 8
HEAD_DIM = 128
GDN_CONV = 4
FFN_CONV = 3
CHUNK = 64
PAD_ROWS = 8
TILE_ROWS = 256
VMEM_LIMIT = 56 * 1024 * 1024


def _params(*sem):
    return pltpu.CompilerParams(dimension_semantics=sem, vmem_limit_bytes=VMEM_LIMIT)


def _sigmoid(x):
    return jax.nn.sigmoid(x)


def _silu(x):
    return x * jax.nn.sigmoid(x)


def _softplus(x):
    return jnp.maximum(x, 0.0) + jnp.log1p(jnp.exp(-jnp.abs(x)))


def _rms(x, g):
    return x * lax.rsqrt(jnp.mean(x * x, axis=-1, keepdims=True) + EPS) * g


def _dot(a, b):
    return jnp.dot(a.astype(BF16), b.astype(BF16), preferred_element_type=F32)


def _dot_nt(a, b):
    return lax.dot_general(a.astype(BF16), b.astype(BF16), (((1,), (1,)), ((), ())),
                           preferred_element_type=F32)


def _split2(x):
    hi = x.astype(BF16)
    lo = (x - hi.astype(F32)).astype(BF16)
    return hi, lo


def _dot_hi(a, b):
    ah, al = _split2(a)
    bh, bl = _split2(b)
    d = functools.partial(jnp.dot, preferred_element_type=F32)
    return d(ah, bh) + (d(ah, bl) + d(al, bh))


def _dot_sel(sel, x):
    hi = x.astype(BF16)
    r1 = x - hi.astype(F32)
    mid = r1.astype(BF16)
    lo = (r1 - mid.astype(F32)).astype(BF16)
    d = functools.partial(jnp.dot, preferred_element_type=F32)
    return d(sel, hi) + (d(sel, mid) + d(sel, lo))


def _idiv(x, p2):
    return x >> (p2.bit_length() - 1)


def _unit_lower_inverse(a, block, span):
    n = a.shape[0]
    row = lax.broadcasted_iota(jnp.int32, (n, n), 0)
    col = lax.broadcasted_iota(jnp.int32, (n, n), 1)
    eye = (row == col).astype(F32)
    if block >= span:
        return _neumann(eye, a, span)
    diag = _idiv(row, block) == _idiv(col, block)
    inv = _neumann(eye, jnp.where(diag, a, 0.0), block)
    blk = _neumann(eye, _dot_hi(inv, jnp.where(diag, 0.0, a)), span // block)
    return _dot_hi(blk, inv)


def _neumann(eye, d, order):
    p = eye - d
    dp = d
    k = 2
    while k < order:
        dp = _dot_hi(dp, dp)
        p = p + _dot_hi(p, dp)
        k *= 2
    return p


def _mod_kernel(c_ref, w_ref, b_ref, o_ref):
    cs = _silu(c_ref[...])
    o_ref[0] = _dot(cs, w_ref[0]) + b_ref[0]


def _mod_call(c_all, ada_w, ada_b):
    depth, d, n = ada_w.shape
    rows = c_all.shape[0]
    tn = n // 4
    return pl.pallas_call(
        _mod_kernel,
        grid=(depth, n // tn),
        in_specs=[pl.BlockSpec((rows, d), lambda l, j: (0, 0)),
                  pl.BlockSpec((1, d, tn), lambda l, j: (l, 0, j)),
                  pl.BlockSpec((1, 1, tn), lambda l, j: (l, 0, j))],
        out_specs=pl.BlockSpec((1, rows, tn), lambda l, j: (l, 0, j)),
        out_shape=jax.ShapeDtypeStruct((depth, rows, n), F32),
        compiler_params=_params("arbitrary", "arbitrary"),
        name="adaln_mod",
    )(c_all, ada_w, ada_b.reshape(depth, 1, n))


def _inproj_kernel(x_ref, sh_ref, sc_ref, g_ref, w_ref, *out_refs, splits):
    nb, lb, d = x_ref.shape
    h = _rms(x_ref[...], g_ref[...]) * (1.0 + sc_ref[...]) + sh_ref[...]
    hb = h.reshape(nb * lb, d).astype(BF16)
    off = 0
    for o_ref, n in zip(out_refs, splits):
        o_ref[...] = jnp.dot(hb, w_ref[:, off:off + n], preferred_element_type=F32).reshape(nb, lb, n)
        off += n


def _inproj_call(x, mod, gain, w, splits, nb, lb):
    bsz, seq, d = x.shape
    ntot = w.shape[1]
    row = lambda k: pl.BlockSpec((nb, 1, d), lambda b, t, k=k: (b, 0, k))
    return pl.pallas_call(
        functools.partial(_inproj_kernel, splits=splits),
        grid=(bsz // nb, seq // lb),
        in_specs=[pl.BlockSpec((nb, lb, d), lambda b, t: (b, t, 0)),
                  row(0), row(1),
                  pl.BlockSpec((1, d), lambda b, t: (0, 0)),
                  pl.BlockSpec((d, ntot), lambda b, t: (0, 0), pipeline_mode=pl.Buffered(1))],
        out_specs=[pl.BlockSpec((nb, lb, n), lambda b, t: (b, t, 0)) for n in splits],
        out_shape=[jax.ShapeDtypeStruct((bsz, seq, n), F32) for n in splits],
        compiler_params=_params("arbitrary", "arbitrary"),
        name="inproj",
    )(x, mod, mod, gain.reshape(1, d), w)


def _ffn_kernel(og_ref, x_ref, g1_ref, sh2_ref, sc2_ref, g2_ref, npost_ref, npre_ref, npostf_ref,
                wout_ref, wg_ref, wu_ref, cw_ref, cb_ref, wd_ref, fbuf_ref,
                xo_ref, fbnew_ref, gbuf):
    nb, lb, d = x_ref.shape
    f = wg_ref.shape[1]
    m = nb * lb
    t = pl.program_id(1)

    mix = jnp.dot(og_ref[...].reshape(m, d).astype(BF16), wout_ref[...], preferred_element_type=F32)
    x1 = x_ref[...] + (1.0 + g1_ref[...]) * _rms(mix.reshape(nb, lb, d), npost_ref[...])
    h = _rms(x1, npre_ref[...]) * (1.0 + sc2_ref[...]) + sh2_ref[...]
    hb = h.reshape(m, d).astype(BF16)
    gt = jnp.dot(hb, wg_ref[...], preferred_element_type=F32).reshape(nb, lb, f)
    up = jnp.dot(hb, wu_ref[...], preferred_element_type=F32)

    @pl.when(t == 0)
    def _():
        gbuf[:, 0:PAD_ROWS, :] = fbuf_ref[...]

    @pl.when(t > 0)
    def _():
        gbuf[:, 0:PAD_ROWS, :] = gbuf[:, lb:lb + PAD_ROWS, :]

    gbuf[:, PAD_ROWS:PAD_ROWS + lb, :] = gt
    fbnew_ref[...] = gbuf[:, lb:lb + PAD_ROWS, :]

    cw = cw_ref[...]
    y = cb_ref[...] + gt * cw[FFN_CONV - 1:FFN_CONV]
    for j in range(FFN_CONV - 1):
        s = PAD_ROWS - (FFN_CONV - 1) + j
        y = y + gbuf[:, s:s + lb, :] * cw[j:j + 1]
    act = _silu(y).reshape(m, f) * up
    out = jnp.dot(act.astype(BF16), wd_ref[...], preferred_element_type=F32)
    xo_ref[...] = x1 + (1.0 + g2_ref[...]) * _rms(out.reshape(nb, lb, d), npostf_ref[...])


def _ffn_call(og, x, mod, npost, npre, npostf, wout, wg, wu, cw, cb, wd, fbuf, nb, lb):
    bsz, seq, d = x.shape
    f = wg.shape[1]
    row = lambda k: pl.BlockSpec((nb, 1, d), lambda b, t, k=k: (b, 0, k))
    vec = lambda n: pl.BlockSpec((1, n), lambda b, t: (0, 0))
    const = lambda r, c: pl.BlockSpec((r, c), lambda b, t: (0, 0), pipeline_mode=pl.Buffered(1))
    tok = pl.BlockSpec((nb, lb, d), lambda b, t: (b, t, 0))
    hist = pl.BlockSpec((nb, PAD_ROWS, f), lambda b, t: (b, 0, 0))
    return pl.pallas_call(
        _ffn_kernel,
        grid=(bsz // nb, seq // lb),
        in_specs=[tok, tok, row(2), row(3), row(4), row(5), vec(d), vec(d), vec(d),
                  const(d, d), const(d, f), const(d, f),
                  pl.BlockSpec((FFN_CONV, f), lambda b, t: (0, 0)), vec(f), const(f, d), hist],
        out_specs=[tok, hist],
        out_shape=[jax.ShapeDtypeStruct((bsz, seq, d), F32),
                   jax.ShapeDtypeStruct((bsz, PAD_ROWS, f), F32)],
        scratch_shapes=[pltpu.VMEM((nb, lb + PAD_ROWS, f), F32)],
        compiler_params=_params("arbitrary", "arbitrary"),
        name="ffn_block",
    )(og, x, mod, mod, mod, mod, npost.reshape(1, d), npre.reshape(1, d), npostf.reshape(1, d),
      wout, wg, wu, cw, cb.reshape(1, f), wd, fbuf)


def _seq_masks(nsq, lc):
    c = nsq * lc
    i = np.arange(c)[:, None]
    j = np.arange(c)[None, :]
    same = (i // lc) == (j // lc)
    return i, j, same


def _gdn_consts(nsq, lc):
    i, j, same = _seq_masks(nsq, lc)
    return jnp.asarray((same & (j <= i)).astype(np.float32), BF16)


def _hgrn_levels(lc):
    s, out = lc, []
    while s >= 2:
        out.append(s)
        s //= 2
    return out


def _hgrn_consts(nsq, lc):
    i, j, same = _seq_masks(nsq, lc)
    mats = [same & (j <= i), same & (j > i)]
    for s in _hgrn_levels(lc):
        mid = (i // s) * s + s // 2
        mats.append((i >= mid) & (j >= mid) & (j <= i))
        mats.append((i < mid) & (j > i) & (j < mid))
    return jnp.asarray(np.concatenate(mats, axis=0).astype(np.float32), BF16)


def _gdn_kernel(qkv_ref, gt_ref, ba_ref, s0_ref, cbuf_ref, cw_ref, cb_ref, alog_ref, dtb_ref, ng_ref,
                tri_ref, og_ref, snew_ref, cbnew_ref, xbuf, s_scr, *, nsq, lc):
    c = nsq * lc
    hd = HEAD_DIM
    qk = HEADS * hd
    t = pl.program_id(1)
    last = pl.num_programs(1) - 1

    @pl.when(t == 0)
    def _():
        xbuf[:, 0:PAD_ROWS, :] = cbuf_ref[...]
        s_scr[...] = s0_ref[...]

    @pl.when(t > 0)
    def _():
        xbuf[:, 0:PAD_ROWS, :] = xbuf[:, lc:lc + PAD_ROWS, :]

    xbuf[:, PAD_ROWS:PAD_ROWS + lc, :] = qkv_ref[...]
    cbnew_ref[...] = xbuf[:, lc:lc + PAD_ROWS, :]

    cw = cw_ref[...]
    y = cb_ref[...] + xbuf[:, PAD_ROWS:PAD_ROWS + lc, :] * cw[GDN_CONV - 1:GDN_CONV]
    for j in range(GDN_CONV - 1):
        s = PAD_ROWS - (GDN_CONV - 1) + j
        y = y + xbuf[:, s:s + lc, :] * cw[j:j + 1]
    qkv = _silu(y).reshape(c, 3 * qk)
    gt = gt_ref[...].reshape(c, qk)

    ba = ba_ref[...].reshape(c, hd)
    beta = _sigmoid(ba)
    g = -jnp.exp(alog_ref[...]) * _softplus(ba + dtb_ref[...])
    gcum = _dot_sel(tri_ref[...], g)
    gcum_t = gcum.T

    row = lax.broadcasted_iota(jnp.int32, (c, c), 0)
    col = lax.broadcasted_iota(jnp.int32, (c, c), 1)
    same = _idiv(row, lc) == _idiv(col, lc)
    incl = same & (col <= row)
    strict = same & (col < row)
    rid = _idiv(lax.broadcasted_iota(jnp.int32, (c, 1), 0), lc)

    for h in range(HEADS):
        qh = qkv[:, h * hd:(h + 1) * hd]
        kh = qkv[:, qk + h * hd:qk + (h + 1) * hd]
        vh = qkv[:, 2 * qk + h * hd:2 * qk + (h + 1) * hd]
        qh = qh * (lax.rsqrt(jnp.sum(qh * qh, axis=-1, keepdims=True) + EPS) * (hd ** -0.5))
        kh = kh * lax.rsqrt(jnp.sum(kh * kh, axis=-1, keepdims=True) + EPS)
        bcol = beta[:, h:h + 1]
        gc = gcum[:, HEADS + h:HEADS + h + 1]
        gr = gcum_t[HEADS + h:HEADS + h + 1, :]
        decay = jnp.where(incl, jnp.exp(jnp.where(incl, gc - gr, 0.0)), 0.0)
        eg = jnp.exp(gc)
        kb = kh * bcol
        a = jnp.where(strict, _dot_nt(kb, kh) * decay, 0.0)
        tinv = _unit_lower_inverse(a, min(16, lc), lc)
        rhs = jnp.concatenate([vh * bcol, kb * eg], axis=1)
        sol = _dot_hi(tinv, rhs)
        uv, w = sol[:, :hd], sol[:, hd:]
        attn = jnp.where(incl, _dot_nt(qh, kh) * decay, 0.0)
        qg = qh * eg

        us, os_ = [], []
        for n in range(nsq):
            sl = slice(n * lc, (n + 1) * lc)
            st = s_scr[n, h]
            us.append(uv[sl] - _dot(w[sl], st))
            os_.append(_dot(qg[sl], st))
        u = us[0] if nsq == 1 else jnp.concatenate(us, axis=0)
        o = (os_[0] if nsq == 1 else jnp.concatenate(os_, axis=0)) + _dot(attn, u)

        for n in range(nsq):
            gl = gcum[(n + 1) * lc - 1:(n + 1) * lc, HEADS + h:HEADS + h + 1]
            if nsq > 1:
                kd = jnp.where(rid == n, kh * jnp.exp(jnp.where(rid == n, gl - gc, 0.0)), 0.0)
            else:
                kd = kh * jnp.exp(gl - gc)
            s_scr[n, h] = s_scr[n, h] * jnp.exp(gl) + _dot(kd.T, u)

        on = _rms(o, ng_ref[...])
        og_ref[:, :, h * hd:(h + 1) * hd] = (on * _silu(gt[:, h * hd:(h + 1) * hd])).reshape(nsq, lc, hd)

    @pl.when(t == last)
    def _():
        snew_ref[...] = s_scr[...]


def _gdn_call(qkv, gt, ba, s0, cbuf, cw, cb, alog, dtb, ng, nsq, lc):
    bsz, seq, w3 = qkv.shape
    qk = w3 // 3
    hd = HEAD_DIM
    tri = _gdn_consts(nsq, lc)
    c = nsq * lc
    tok = lambda n: pl.BlockSpec((nsq, lc, n), lambda b, t: (b, t, 0))
    vec = lambda n: pl.BlockSpec((1, n), lambda b, t: (0, 0))
    state = pl.BlockSpec((nsq, HEADS, hd, hd), lambda b, t: (b, 0, 0, 0))
    hist = pl.BlockSpec((nsq, PAD_ROWS, w3), lambda b, t: (b, 0, 0))
    return pl.pallas_call(
        functools.partial(_gdn_kernel, nsq=nsq, lc=lc),
        grid=(bsz // nsq, seq // lc),
        in_specs=[tok(w3), tok(qk), tok(hd), state, hist,
                  pl.BlockSpec((GDN_CONV, w3), lambda b, t: (0, 0)), vec(w3), vec(hd), vec(hd), vec(hd),
                  pl.BlockSpec((c, c), lambda b, t: (0, 0))],
        out_specs=[tok(qk), state, hist],
        out_shape=[jax.ShapeDtypeStruct((bsz, seq, qk), F32),
                   jax.ShapeDtypeStruct(s0.shape, F32),
                   jax.ShapeDtypeStruct((bsz, PAD_ROWS, w3), F32)],
        scratch_shapes=[pltpu.VMEM((nsq, lc + PAD_ROWS, w3), F32),
                        pltpu.VMEM((nsq, HEADS, hd, hd), F32)],
        compiler_params=_params("arbitrary", "arbitrary"),
        name="gdn_core",
    )(qkv, gt, ba, s0, cbuf, cw, cb.reshape(1, w3), alog, dtb, ng.reshape(1, hd), tri)


def _hgrn_kernel(proj_ref, s0_ref, lbraw_ref, ng_ref, cm_ref, og_ref, snew_ref, st_scr, *, nsq, lc, layer):
    c = nsq * lc
    hd = HEAD_DIM
    qf = HEADS * hd
    t = pl.program_id(1)
    last = pl.num_programs(1) - 1

    @pl.when(t == 0)
    def _():
        for n in range(nsq):
            for h in range(HEADS):
                st_scr[n, h] = s0_ref[n, h].T

    lbraw = lbraw_ref[...]
    e = jnp.exp(lbraw - jnp.max(lbraw, axis=0, keepdims=True))
    p = e / jnp.sum(e, axis=0, keepdims=True)
    lb = jnp.zeros((1, qf), F32)
    for m in range(1, layer + 1):
        lb = lb + p[m:m + 1, :]

    proj = proj_ref[...].reshape(c, 4 * qf)
    q = _silu(proj[:, :qf])
    fr = proj[:, qf:2 * qf]
    v = proj[:, 2 * qf:3 * qf]
    gt = proj[:, 3 * qf:]
    log_sig = jnp.minimum(fr, 0.0) - jnp.log1p(jnp.exp(-jnp.abs(fr)))
    t1 = jnp.log(jnp.maximum(lb, LB_FLOOR))
    t2 = jnp.log1p(-lb) + log_sig
    logf = jnp.maximum(t1, t2) + jnp.log1p(jnp.exp(-jnp.abs(t1 - t2)))
    k = (1.0 - lb) * _sigmoid(-fr)

    ex = jnp.exp(_dot_sel(cm_ref[...], logf))
    levels = _hgrn_levels(lc)

    row = lax.broadcasted_iota(jnp.int32, (c, c), 0)
    col = lax.broadcasted_iota(jnp.int32, (c, c), 1)
    masks = [row == col]
    for s in levels:
        masks.append((_idiv(row, s) == _idiv(col, s)) & ((row & (s - 1)) >= s // 2) & ((col & (s - 1)) < s // 2))
    cid = _idiv(lax.broadcasted_iota(jnp.int32, (1, c), 1), lc)

    for h in range(HEADS):
        hs = slice(h * hd, (h + 1) * hd)
        qh, kh, vh = q[:, hs], k[:, hs], v[:, hs]
        eg = ex[0:c, hs]
        ekd = ex[c:2 * c, hs]
        attn = jnp.where(masks[0], _dot_nt(qh, kh), 0.0)
        for li in range(len(levels)):
            eq = ex[(2 + 2 * li) * c:(3 + 2 * li) * c, hs]
            ek = ex[(3 + 2 * li) * c:(4 + 2 * li) * c, hs]
            attn = attn + jnp.where(masks[1 + li], _dot_nt(qh * eq, kh * ek), 0.0)
        qe = qh * eg
        kd = kh * ekd
        vt = vh.T
        os_ = []
        for n in range(nsq):
            sl = slice(n * lc, (n + 1) * lc)
            st = st_scr[n, h]
            os_.append(_dot_nt(qe[sl], st))
            vtn = vt if nsq == 1 else jnp.where(cid == n, vt, 0.0)
            egl = eg[(n + 1) * lc - 1:(n + 1) * lc, :]
            st_scr[n, h] = st * egl + _dot(vtn, kd)
        o = (os_[0] if nsq == 1 else jnp.concatenate(os_, axis=0)) + _dot(attn, vh)
        on = _rms(o, ng_ref[...])
        og_ref[:, :, hs] = (on * _silu(gt[:, hs])).reshape(nsq, lc, hd)

    @pl.when(t == last)
    def _():
        for n in range(nsq):
            for h in range(HEADS):
                snew_ref[n, h] = st_scr[n, h].T


def _hgrn_call(proj, s0, lbraw, ng, layer, nsq, lc):
    bsz, seq, w4 = proj.shape
    qf = w4 // 4
    hd = HEAD_DIM
    cm = _hgrn_consts(nsq, lc)
    tok = lambda n: pl.BlockSpec((nsq, lc, n), lambda b, t: (b, t, 0))
    state = pl.BlockSpec((nsq, HEADS, hd, hd), lambda b, t: (b, 0, 0, 0))
    return pl.pallas_call(
        functools.partial(_hgrn_kernel, nsq=nsq, lc=lc, layer=layer),
        grid=(bsz // nsq, seq // lc),
        in_specs=[tok(w4), state,
                  pl.BlockSpec(lbraw.shape, lambda b, t: (0, 0)),
                  pl.BlockSpec((1, hd), lambda b, t: (0, 0)),
                  pl.BlockSpec(cm.shape, lambda b, t: (0, 0))],
        out_specs=[tok(qf), state],
        out_shape=[jax.ShapeDtypeStruct((bsz, seq, qf), F32),
                   jax.ShapeDtypeStruct(s0.shape, F32)],
        scratch_shapes=[pltpu.VMEM((nsq, HEADS, hd, hd), F32)],
        compiler_params=_params("arbitrary", "arbitrary"),
        name="hgrn_core",
    )(proj, s0, lbraw, ng.reshape(1, hd), cm)


def _prep_weights(gdn_w_in, gdn_a_log, gdn_dt_bias, hgrn_w_in, gdn_w_out, hgrn_w_out, ffn_w_gu, ffn_w_down):
    d = gdn_w_in.shape[1]
    qk = HEADS * HEAD_DIM
    main = 4 * qk
    pad = HEAD_DIM - 2 * HEADS
    w_gdn = jnp.pad(gdn_w_in, ((0, 0), (0, 0), (0, pad))).astype(BF16)
    gate_pad = lambda a: jnp.pad(a, ((0, 0), (HEADS, HEAD_DIM - 2 * HEADS)))[:, None, :]
    f = ffn_w_down.shape[1]
    return dict(
        w_gdn=w_gdn, alog=gate_pad(gdn_a_log), dtb=gate_pad(gdn_dt_bias),
        w_hgrn=hgrn_w_in.astype(BF16), gdn_w_out=gdn_w_out.astype(BF16), hgrn_w_out=hgrn_w_out.astype(BF16),
        wg=ffn_w_gu[:, :, :f].astype(BF16), wu=ffn_w_gu[:, :, f:].astype(BF16), wd=ffn_w_down.astype(BF16),
        main=main, d=d)


def _trunk(x, mod, s_gdn, s_gconv, s_hgrn, s_fconv, wts, p, nb, lb, nsq, lc):
    depth = mod.shape[0]
    bsz = x.shape[0]
    qk = HEADS * HEAD_DIM
    new_gdn, new_gconv, new_hgrn, new_fconv = [], [], [], []
    for layer in range(depth):
        modl = mod[layer][:, None, :]
        j = layer // N_MIXERS
        if layer % N_MIXERS == 0:
            qkv, gt, ba = _inproj_call(x, modl, wts["norm_pre_mix"][layer], p["w_gdn"][j],
                                       (3 * qk, qk, HEAD_DIM), nb, lb)
            cbuf = jnp.pad(s_gconv[j], ((0, 0), (PAD_ROWS - (GDN_CONV - 1), 0), (0, 0)))
            og, s_new, cb_new = _gdn_call(qkv, gt, ba, s_gdn[j], cbuf, wts["gdn_conv_w"][j], wts["gdn_conv_b"][j],
                                          p["alog"][j], p["dtb"][j], wts["gdn_norm"][j], nsq, lc)
            new_gdn.append(s_new)
            new_gconv.append(cb_new[:, PAD_ROWS - (GDN_CONV - 1):, :])
            w_out = p["gdn_w_out"][j]
        else:
            (proj,) = _inproj_call(x, modl, wts["norm_pre_mix"][layer], p["w_hgrn"][j], (4 * qk,), nb, lb)
            og, s_new = _hgrn_call(proj, s_hgrn[j], wts["hgrn_lb"], wts["hgrn_norm"][j], j, nsq, lc)
            new_hgrn.append(s_new)
            w_out = p["hgrn_w_out"][j]
        fbuf = jnp.pad(s_fconv[layer], ((0, 0), (PAD_ROWS - (FFN_CONV - 1), 0), (0, 0)))
        x, fb_new = _ffn_call(og, x, modl, wts["norm_post_mix"][layer], wts["norm_pre_ffn"][layer],
                              wts["norm_post_ffn"][layer], w_out, p["wg"][layer], p["wu"][layer],
                              wts["ffn_conv_w"][layer], wts["ffn_conv_b"][layer], p["wd"][layer], fbuf, nb, lb)
        new_fconv.append(fb_new[:, PAD_ROWS - (FFN_CONV - 1):, :])
    return x, jnp.stack(new_gdn), jnp.stack(new_gconv), jnp.stack(new_hgrn), jnp.stack(new_fconv)


def kernel(x_prompt, x_sample, state_gdn, state_gdn_conv, state_hgrn, state_ffn_conv, c_prompt, c_sample, ada_w, ada_b, norm_pre_mix, norm_post_mix, norm_pre_ffn, norm_post_ffn, gdn_w_in, gdn_conv_w, gdn_conv_b, gdn_a_log, gdn_dt_bias, gdn_norm, gdn_w_out, hgrn_lb, hgrn_w_in, hgrn_norm, hgrn_w_out, ffn_w_gu, ffn_conv_w, ffn_conv_b, ffn_w_down):
    wts = dict(norm_pre_mix=norm_pre_mix, norm_post_mix=norm_post_mix, norm_pre_ffn=norm_pre_ffn,
               norm_post_ffn=norm_post_ffn, gdn_conv_w=gdn_conv_w, gdn_conv_b=gdn_conv_b, gdn_norm=gdn_norm,
               hgrn_lb=hgrn_lb, hgrn_norm=hgrn_norm, ffn_conv_w=ffn_conv_w, ffn_conv_b=ffn_conv_b)
    p = _prep_weights(gdn_w_in, gdn_a_log, gdn_dt_bias, hgrn_w_in, gdn_w_out, hgrn_w_out, ffn_w_gu, ffn_w_down)

    bp, sp, _ = x_prompt.shape
    bs, ss, _ = x_sample.shape
    mod = _mod_call(jnp.concatenate([c_prompt, c_sample], axis=0), ada_w, ada_b)
    mod_p, mod_s = mod[:, :bp], mod[:, bp:]

    dt = x_prompt.dtype
    zeros = lambda a: jnp.zeros((a.shape[0], bp) + a.shape[2:], dt)
    y_p, p_gdn, p_gconv, p_hgrn, p_fconv = _trunk(
        x_prompt, mod_p, zeros(state_gdn), zeros(state_gdn_conv), zeros(state_hgrn), zeros(state_ffn_conv),
        wts, p, nb=1, lb=min(TILE_ROWS, sp), nsq=1, lc=min(CHUNK, sp))
    y_s, s_gdn, s_gconv, s_hgrn, s_fconv = _trunk(
        x_sample, mod_s, state_gdn, state_gdn_conv, state_hgrn, state_ffn_conv,
        wts, p, nb=min(bs, TILE_ROWS // ss), lb=ss, nsq=min(bs, CHUNK // ss), lc=ss)
    return (y_p, y_s, p_gdn, p_gconv, p_hgrn, p_fconv, s_gdn, s_gconv, s_hgrn, s_fconv)
```

```python
import functools

import numpy as np
import jax
import jax.numpy as jnp
from jax import lax
from jax.experimental import pallas as pl
from jax.experimental.pallas import tpu as pltpu

F32 = jnp.float32
BF16 = jnp.bfloat16
EPS = 1e-6
LB_FLOOR = 1e-30
N_MOD = 6
N_MIXERS = 2
HEADS = 8
HEAD_DIM = 128
GDN_CONV = 4
FFN_CONV = 3
CHUNK = 64
PAD_ROWS = 8
TILE_ROWS = 512
SHORT_TILE_ROWS = 256
GDN_CHUNKS = 4
HGRN_CHUNKS = 4
VMEM_LIMIT = 56 * 1024 * 1024
STATE_VMEM_BUDGET = 24 * 1024 * 1024


def _params(*sem):
    return pltpu.CompilerParams(dimension_semantics=sem, vmem_limit_bytes=VMEM_LIMIT)


def _sigmoid(x):
    return jax.nn.sigmoid(x)


def _silu(x):
    return x * jax.nn.sigmoid(x)


def _softplus(x):
    return jnp.maximum(x, 0.0) + jnp.log1p(jnp.exp(-jnp.abs(x)))


def _rms(x, g):
    return x * lax.rsqrt(jnp.mean(x * x, axis=-1, keepdims=True) + EPS) * g


def _dot(a, b):
    return jnp.dot(a.astype(BF16), b.astype(BF16), preferred_element_type=F32)


def _bmm(a, b):
    return lax.dot_general(a.astype(BF16), b.astype(BF16), (((2,), (1,)), ((0,), (0,))),
                           preferred_element_type=F32)


def _bmm_nt(a, b):
    return lax.dot_general(a.astype(BF16), b.astype(BF16), (((2,), (2,)), ((0,), (0,))),
                           preferred_element_type=F32)


def _split2(x):
    hi = x.astype(BF16)
    lo = (x - hi.astype(F32)).astype(BF16)
    return hi, lo


def _bmm_hi(a, b):
    ah, al = _split2(a)
    bh, bl = _split2(b)
    return _bmm(ah, bh) + (_bmm(ah, bl) + _bmm(al, bh))


def _dot_sel(sel, x, pieces):
    acc = None
    r = x
    for i in range(pieces):
        part = r.astype(BF16)
        if i + 1 < pieces:
            r = r - part.astype(F32)
        term = jnp.dot(sel, part, preferred_element_type=F32)
        acc = term if acc is None else acc + term
    return acc


def _idiv(x, p2):
    return x >> (p2.bit_length() - 1)


def _unit_lower_inverse(a, block, span):
    n = a.shape[-1]
    row = lax.broadcasted_iota(jnp.int32, (n, n), 0)
    col = lax.broadcasted_iota(jnp.int32, (n, n), 1)
    eye = (row == col).astype(F32)[None]
    if block >= span:
        return _neumann(eye, a, span)
    diag = (_idiv(row, block) == _idiv(col, block))[None]
    inv = _neumann(eye, jnp.where(diag, a, 0.0), block)
    blk = _neumann(eye, _bmm_hi(inv, jnp.where(diag, 0.0, a)), span // block)
    return _bmm_hi(blk, inv)


def _neumann(eye, d, order):
    p = eye - d
    dp = d
    k = 2
    while k < order:
        dp = _bmm_hi(dp, dp)
        p = p + _bmm_hi(p, dp)
        k *= 2
    return p


def _items(x, nch, base, width):
    return jnp.concatenate(
        [x[ch * CHUNK:(ch + 1) * CHUNK, base + h * width:base + (h + 1) * width][None]
         for ch in range(nch) for h in range(HEADS)], axis=0)


def _mod_kernel(c_ref, w_ref, b_ref, o_ref):
    cs = _silu(c_ref[...])
    o_ref[0] = _dot(cs, w_ref[0]) + b_ref[0]


def _mod_call(c_all, ada_w, ada_b):
    depth, d, n = ada_w.shape
    rows = c_all.shape[0]
    tn = n // 4
    return pl.pallas_call(
        _mod_kernel,
        grid=(depth, n // tn),
        in_specs=[pl.BlockSpec((rows, d), lambda l, j: (0, 0)),
                  pl.BlockSpec((1, d, tn), lambda l, j: (l, 0, j)),
                  pl.BlockSpec((1, 1, tn), lambda l, j: (l, 0, j))],
        out_specs=pl.BlockSpec((1, rows, tn), lambda l, j: (l, 0, j)),
        out_shape=jax.ShapeDtypeStruct((depth, rows, n), F32),
        compiler_params=_params("arbitrary", "arbitrary"),
        name="adaln_mod",
    )(c_all, ada_w, ada_b.reshape(depth, 1, n))


def _inproj_kernel(x_ref, sh_ref, sc_ref, g_ref, w_ref, *out_refs, splits):
    nb, lb, d = x_ref.shape
    h = _rms(x_ref[...], g_ref[...]) * (1.0 + sc_ref[...]) + sh_ref[...]
    hb = h.reshape(nb * lb, d).astype(BF16)
    off = 0
    for o_ref, n in zip(out_refs, splits):
        o_ref[...] = jnp.dot(hb, w_ref[:, off:off + n], preferred_element_type=F32).reshape(nb, lb, n)
        off += n


def _inproj_call(x, mod, gain, w, splits, nb, lb):
    bsz, seq, d = x.shape
    ntot = w.shape[1]
    row = lambda k: pl.BlockSpec((nb, 1, d), lambda b, t, k=k: (b, 0, k))
    return pl.pallas_call(
        functools.partial(_inproj_kernel, splits=splits),
        grid=(bsz // nb, seq // lb),
        in_specs=[pl.BlockSpec((nb, lb, d), lambda b, t: (b, t, 0)),
                  row(0), row(1),
                  pl.BlockSpec((1, d), lambda b, t: (0, 0)),
                  pl.BlockSpec((d, ntot), lambda b, t: (0, 0), pipeline_mode=pl.Buffered(1))],
        out_specs=[pl.BlockSpec((nb, lb, n), lambda b, t: (b, t, 0)) for n in splits],
        out_shape=[jax.ShapeDtypeStruct((bsz, seq, n), F32) for n in splits],
        compiler_params=_params("arbitrary", "arbitrary"),
        name="inproj",
    )(x, mod, mod, gain.reshape(1, d), w)


def _ffn_kernel(og_ref, x_ref, g1_ref, sh2_ref, sc2_ref, g2_ref, npost_ref, npre_ref, npostf_ref,
                wout_ref, wg_ref, wu_ref, cw_ref, cb_ref, wd_ref, fbuf_ref,
                xo_ref, fbnew_ref, gbuf):
    nb, lb, d = x_ref.shape
    f = wg_ref.shape[1]
    m = nb * lb
    t = pl.program_id(1)

    mix = jnp.dot(og_ref[...].reshape(m, d).astype(BF16), wout_ref[...], preferred_element_type=F32)
    x1 = x_ref[...] + (1.0 + g1_ref[...]) * _rms(mix.reshape(nb, lb, d), npost_ref[...])
    h = _rms(x1, npre_ref[...]) * (1.0 + sc2_ref[...]) + sh2_ref[...]
    hb = h.reshape(m, d).astype(BF16)
    gt = jnp.dot(hb, wg_ref[...], preferred_element_type=F32).reshape(nb, lb, f)
    up = jnp.dot(hb, wu_ref[...], preferred_element_type=F32)

    @pl.when(t == 0)
    def _():
        gbuf[:, 0:PAD_ROWS, :] = fbuf_ref[...]

    @pl.when(t > 0)
    def _():
        gbuf[:, 0:PAD_ROWS, :] = gbuf[:, lb:lb + PAD_ROWS, :]

    gbuf[:, PAD_ROWS:PAD_ROWS + lb, :] = gt
    fbnew_ref[...] = gbuf[:, lb:lb + PAD_ROWS, :]

    cw = cw_ref[...]
    y = cb_ref[...] + gt * cw[FFN_CONV - 1:FFN_CONV]
    for j in range(FFN_CONV - 1):
        s = PAD_ROWS - (FFN_CONV - 1) + j
        y = y + gbuf[:, s:s + lb, :] * cw[j:j + 1]
    act = _silu(y).reshape(m, f) * up
    out = jnp.dot(act.astype(BF16), wd_ref[...], preferred_element_type=F32)
    xo_ref[...] = x1 + (1.0 + g2_ref[...]) * _rms(out.reshape(nb, lb, d), npostf_ref[...])


def _ffn_call(og, x, mod, npost, npre, npostf, wout, wg, wu, cw, cb, wd, fbuf, nb, lb):
    bsz, seq, d = x.shape
    f = wg.shape[1]
    row = lambda k: pl.BlockSpec((nb, 1, d), lambda b, t, k=k: (b, 0, k))
    vec = lambda n: pl.BlockSpec((1, n), lambda b, t: (0, 0))
    const = lambda r, c: pl.BlockSpec((r, c), lambda b, t: (0, 0), pipeline_mode=pl.Buffered(1))
    tok = pl.BlockSpec((nb, lb, d), lambda b, t: (b, t, 0))
    hist = pl.BlockSpec((nb, PAD_ROWS, f), lambda b, t: (b, 0, 0))
    return pl.pallas_call(
        _ffn_kernel,
        grid=(bsz // nb, seq // lb),
        in_specs=[tok, tok, row(2), row(3), row(4), row(5), vec(d), vec(d), vec(d),
                  const(d, d), const(d, f), const(d, f),
                  pl.BlockSpec((FFN_CONV, f), lambda b, t: (0, 0)), vec(f), const(f, d), hist],
        out_specs=[tok, hist],
        out_shape=[jax.ShapeDtypeStruct((bsz, seq, d), F32),
                   jax.ShapeDtypeStruct((bsz, PAD_ROWS, f), F32)],
        scratch_shapes=[pltpu.VMEM((nb, lb + PAD_ROWS, f), F32)],
        compiler_params=_params("arbitrary", "arbitrary"),
        name="ffn_block",
    )(og, x, mod, mod, mod, mod, npost.reshape(1, d), npre.reshape(1, d), npostf.reshape(1, d),
      wout, wg, wu, cw, cb.reshape(1, f), wd, fbuf)


def _seq_masks(nsq, lc):
    c = nsq * lc
    i = np.arange(c)[:, None]
    j = np.arange(c)[None, :]
    same = (i // lc) == (j // lc)
    return i, j, same


def _gdn_consts(nsq, lc):
    i, j, same = _seq_masks(nsq, lc)
    return jnp.asarray((same & (j <= i)).astype(np.float32), BF16)


def _hgrn_levels(lc):
    s, out = lc, []
    while s >= 2:
        out.append(s)
        s //= 2
    return out


def _hgrn_consts(nsq, lc):
    i, j, same = _seq_masks(nsq, lc)
    mats = [same & (j <= i), same & (j > i)]
    for s in _hgrn_levels(lc):
        mid = (i // s) * s + s // 2
        mats.append(np.where(i >= mid, (j >= mid) & (j <= i), (j > i) & (j < mid)))
    return jnp.asarray(np.concatenate(mats, axis=0).astype(np.float32), BF16)


def _state_index(ch, n, nsq, lc, lr):
    return ((ch * nsq + n) * lc) // lr


def _store_heads(og_ref, ch, nsq, lc, val):
    for h in range(HEADS):
        hs = slice(h * HEAD_DIM, (h + 1) * HEAD_DIM)
        if og_ref.shape[0] == 1:
            og_ref[0, ch * CHUNK:(ch + 1) * CHUNK, hs] = val[h]
        else:
            og_ref[ch * nsq:(ch + 1) * nsq, :, hs] = val[h].reshape(nsq, lc, HEAD_DIM)


def _gdn_kernel(qkv_ref, gt_ref, ba_ref, s0_ref, cbuf_ref, cw_ref, cb_ref, alog_ref, dtb_ref, ng_ref,
                tri_ref, og_ref, snew_ref, cbnew_ref, xbuf, s_scr, *, nch, nsq, lc):
    ns, lr, w3 = qkv_ref.shape
    r = ns * lr
    hd = HEAD_DIM
    qk = HEADS * hd
    t = pl.program_id(1)
    last = pl.num_programs(1) - 1

    @pl.when(t == 0)
    def _():
        xbuf[:, 0:PAD_ROWS, :] = cbuf_ref[...]
        s_scr[...] = s0_ref[...]

    @pl.when(t > 0)
    def _():
        xbuf[:, 0:PAD_ROWS, :] = xbuf[:, lr:lr + PAD_ROWS, :]

    xbuf[:, PAD_ROWS:PAD_ROWS + lr, :] = qkv_ref[...]
    cbnew_ref[...] = xbuf[:, lr:lr + PAD_ROWS, :]

    cw = cw_ref[...]
    y = cb_ref[...] + xbuf[:, PAD_ROWS:PAD_ROWS + lr, :] * cw[GDN_CONV - 1:GDN_CONV]
    for j in range(GDN_CONV - 1):
        s = PAD_ROWS - (GDN_CONV - 1) + j
        y = y + xbuf[:, s:s + lr, :] * cw[j:j + 1]
    qkv = _silu(y).reshape(r, w3)
    gt = gt_ref[...].reshape(r, qk)

    ba = ba_ref[...].reshape(r, hd)
    beta = _sigmoid(ba)
    g = -jnp.exp(alog_ref[...]) * _softplus(ba + dtb_ref[...])
    chunks = [slice(ch * CHUNK, (ch + 1) * CHUNK) for ch in range(nch)]
    gcum = [_dot_sel(tri_ref[...], g[sl], 3) for sl in chunks]
    gcum_t = [x.T for x in gcum]
    per_item = lambda f: jnp.concatenate([f(ch, h)[None] for ch in range(nch) for h in range(HEADS)], axis=0)
    gc = per_item(lambda ch, h: gcum[ch][:, HEADS + h:HEADS + h + 1])
    gr = per_item(lambda ch, h: gcum_t[ch][HEADS + h:HEADS + h + 1, :])
    bcol = per_item(lambda ch, h: beta[chunks[ch], h:h + 1])

    q = _items(qkv, nch, 0, hd)
    k = _items(qkv, nch, qk, hd)
    v = _items(qkv, nch, 2 * qk, hd)
    q = q * (lax.rsqrt(jnp.sum(q * q, axis=-1, keepdims=True) + EPS) * (hd ** -0.5))
    k = k * lax.rsqrt(jnp.sum(k * k, axis=-1, keepdims=True) + EPS)

    row = lax.broadcasted_iota(jnp.int32, (CHUNK, CHUNK), 0)
    col = lax.broadcasted_iota(jnp.int32, (CHUNK, CHUNK), 1)
    same = _idiv(row, lc) == _idiv(col, lc)
    incl = (same & (col <= row))[None]
    strict = (same & (col < row))[None]
    rid = _idiv(lax.broadcasted_iota(jnp.int32, (1, CHUNK, 1), 1), lc)

    decay = jnp.where(incl, jnp.exp(jnp.where(incl, gc - gr, 0.0)), 0.0)
    eg = jnp.exp(gc)
    kb = k * bcol
    a = jnp.where(strict, _bmm_nt(kb, k) * decay, 0.0)
    tinv = _unit_lower_inverse(a, min(16, lc), lc)
    rhs = jnp.concatenate([v * bcol, kb * eg], axis=-1).astype(BF16)
    th, tl = _split2(tinv)
    sol = _bmm(th, rhs) + _bmm(tl, rhs)
    uv, w = sol[..., :hd], sol[..., hd:]
    attn = jnp.where(incl, _bmm_nt(q, k) * decay, 0.0)
    qg = q * eg
    gate = _silu(_items(gt, nch, 0, hd))

    for ch in range(nch):
        it = slice(ch * HEADS, (ch + 1) * HEADS)
        us, os_ = [], []
        for n in range(nsq):
            sl = slice(n * lc, (n + 1) * lc)
            st = s_scr[_state_index(ch, n, nsq, lc, lr)]
            ws = _bmm(jnp.concatenate([w[it, sl], qg[it, sl]], axis=1), st)
            us.append(uv[it, sl] - ws[:, :lc])
            os_.append(ws[:, lc:])
        u = us[0] if nsq == 1 else jnp.concatenate(us, axis=1)
        o = (os_[0] if nsq == 1 else jnp.concatenate(os_, axis=1)) + _bmm(attn[it], u)
        for n in range(nsq):
            si = _state_index(ch, n, nsq, lc, lr)
            gl = gc[it, (n + 1) * lc - 1:(n + 1) * lc, :]
            if nsq > 1:
                kd = jnp.where(rid == n, k[it] * jnp.exp(jnp.where(rid == n, gl - gc[it], 0.0)), 0.0)
            else:
                kd = k[it] * jnp.exp(gl - gc[it])
            s_scr[si] = s_scr[si] * jnp.exp(gl) + _bmm(jnp.swapaxes(kd, 1, 2), u)
        _store_heads(og_ref, ch, nsq, lc, _rms(o, ng_ref[...]) * gate[it])

    @pl.when(t == last)
    def _():
        snew_ref[...] = s_scr[...]


def _gdn_call(qkv, gt, ba, s0, cbuf, cw, cb, alog, dtb, ng, nch, nsq, lc):
    bsz, seq, w3 = qkv.shape
    qk = w3 // 3
    hd = HEAD_DIM
    tri = _gdn_consts(nsq, lc)
    ns, lr = (1, nch * CHUNK) if nsq == 1 else (nch * nsq, lc)
    tok = lambda n: pl.BlockSpec((ns, lr, n), lambda b, t: (b, t, 0))
    vec = lambda n: pl.BlockSpec((1, n), lambda b, t: (0, 0))
    state = pl.BlockSpec((ns, HEADS, hd, hd), lambda b, t: (b, 0, 0, 0))
    hist = pl.BlockSpec((ns, PAD_ROWS, w3), lambda b, t: (b, 0, 0))
    return pl.pallas_call(
        functools.partial(_gdn_kernel, nch=nch, nsq=nsq, lc=lc),
        grid=(bsz // ns, seq // lr),
        in_specs=[tok(w3), tok(qk), tok(hd), state, hist,
                  pl.BlockSpec((GDN_CONV, w3), lambda b, t: (0, 0)), vec(w3), vec(hd), vec(hd), vec(hd),
                  pl.BlockSpec((CHUNK, CHUNK), lambda b, t: (0, 0))],
        out_specs=[tok(qk), state, hist],
        out_shape=[jax.ShapeDtypeStruct((bsz, seq, qk), F32),
                   jax.ShapeDtypeStruct(s0.shape, F32),
                   jax.ShapeDtypeStruct((bsz, PAD_ROWS, w3), F32)],
        scratch_shapes=[pltpu.VMEM((ns, lr + PAD_ROWS, w3), F32),
                        pltpu.VMEM((ns, HEADS, hd, hd), F32)],
        compiler_params=_params("arbitrary", "arbitrary"),
        name="gdn_core",
    )(qkv, gt, ba, s0, cbuf, cw, cb.reshape(1, w3), alog, dtb, ng.reshape(1, hd), tri)


def _hgrn_kernel(proj_ref, s0_ref, lbraw_ref, ng_ref, cm_ref, og_ref, snew_ref, st_scr, *, nch, nsq, lc, layer):
    ns, lr, w4 = proj_ref.shape
    r = ns * lr
    hd = HEAD_DIM
    qf = HEADS * hd
    t = pl.program_id(1)
    last = pl.num_programs(1) - 1

    @pl.when(t == 0)
    def _():
        for n in range(ns):
            st_scr[n] = jnp.swapaxes(s0_ref[n], 1, 2)

    lbraw = lbraw_ref[...]
    e = jnp.exp(lbraw - jnp.max(lbraw, axis=0, keepdims=True))
    p = e / jnp.sum(e, axis=0, keepdims=True)
    lb = jnp.zeros((1, qf), F32)
    for m in range(1, layer + 1):
        lb = lb + p[m:m + 1, :]

    proj = proj_ref[...].reshape(r, w4)
    fr = proj[:, qf:2 * qf]
    log_sig = jnp.minimum(fr, 0.0) - jnp.log1p(jnp.exp(-jnp.abs(fr)))
    t1 = jnp.log(jnp.maximum(lb, LB_FLOOR))
    t2 = jnp.log1p(-lb) + log_sig
    logf = jnp.maximum(t1, t2) + jnp.log1p(jnp.exp(-jnp.abs(t1 - t2)))

    levels = _hgrn_levels(lc)
    nmat = 2 + len(levels)
    cm = cm_ref[...]
    ex = jnp.concatenate([jnp.exp(_dot_sel(cm, logf[ch * CHUNK:(ch + 1) * CHUNK], 2)) for ch in range(nch)], axis=0)
    factor = lambda m: jnp.concatenate(
        [ex[(ch * nmat + m) * CHUNK:(ch * nmat + m + 1) * CHUNK, h * hd:(h + 1) * hd][None]
         for ch in range(nch) for h in range(HEADS)], axis=0)

    q = _silu(_items(proj, nch, 0, hd))
    k = _items((1.0 - lb) * _sigmoid(-fr), nch, 0, hd)
    v = _items(proj, nch, 2 * qf, hd)
    gate = _silu(_items(proj, nch, 3 * qf, hd))

    row = lax.broadcasted_iota(jnp.int32, (CHUNK, CHUNK), 0)
    col = lax.broadcasted_iota(jnp.int32, (CHUNK, CHUNK), 1)
    attn = jnp.where((row == col)[None], _bmm_nt(q, k), 0.0)
    for li, s in enumerate(levels):
        mask = (_idiv(row, s) == _idiv(col, s)) & ((row & (s - 1)) >= s // 2) & ((col & (s - 1)) < s // 2)
        fl = factor(2 + li)
        attn = attn + jnp.where(mask[None], _bmm_nt(q * fl, k * fl), 0.0)
    eg = factor(0)
    qe = q * eg
    kd = k * factor(1)
    cid = _idiv(lax.broadcasted_iota(jnp.int32, (1, 1, CHUNK), 2), lc)

    for ch in range(nch):
        it = slice(ch * HEADS, (ch + 1) * HEADS)
        vt = jnp.swapaxes(v[it], 1, 2)
        os_ = []
        for n in range(nsq):
            sl = slice(n * lc, (n + 1) * lc)
            si = _state_index(ch, n, nsq, lc, lr)
            st = st_scr[si]
            os_.append(_bmm_nt(qe[it, sl], st))
            vtn = vt if nsq == 1 else jnp.where(cid == n, vt, 0.0)
            egl = eg[it, (n + 1) * lc - 1:(n + 1) * lc, :]
            st_scr[si] = st * egl + _bmm(vtn, kd[it])
        o = (os_[0] if nsq == 1 else jnp.concatenate(os_, axis=1)) + _bmm(attn[it], v[it])
        _store_heads(og_ref, ch, nsq, lc, _rms(o, ng_ref[...]) * gate[it])

    @pl.when(t == last)
    def _():
        for n in range(ns):
            snew_ref[n] = jnp.swapaxes(st_scr[n], 1, 2)


def _hgrn_call(proj, s0, lbraw, ng, layer, nch, nsq, lc):
    bsz, seq, w4 = proj.shape
    qf = w4 // 4
    hd = HEAD_DIM
    cm = _hgrn_consts(nsq, lc)
    ns, lr = (1, nch * CHUNK) if nsq == 1 else (nch * nsq, lc)
    tok = lambda n: pl.BlockSpec((ns, lr, n), lambda b, t: (b, t, 0))
    state = pl.BlockSpec((ns, HEADS, hd, hd), lambda b, t: (b, 0, 0, 0))
    return pl.pallas_call(
        functools.partial(_hgrn_kernel, nch=nch, nsq=nsq, lc=lc, layer=layer),
        grid=(bsz // ns, seq // lr),
        in_specs=[tok(w4), state,
                  pl.BlockSpec(lbraw.shape, lambda b, t: (0, 0)),
                  pl.BlockSpec((1, hd), lambda b, t: (0, 0)),
                  pl.BlockSpec(cm.shape, lambda b, t: (0, 0))],
        out_specs=[tok(qf), state],
        out_shape=[jax.ShapeDtypeStruct((bsz, seq, qf), F32),
                   jax.ShapeDtypeStruct(s0.shape, F32)],
        scratch_shapes=[pltpu.VMEM((ns, HEADS, hd, hd), F32)],
        compiler_params=_params("arbitrary", "arbitrary"),
        name="hgrn_core",
    )(proj, s0, lbraw, ng.reshape(1, hd), cm)


def _prep_weights(gdn_w_in, gdn_a_log, gdn_dt_bias, hgrn_w_in, gdn_w_out, hgrn_w_out, ffn_w_gu, ffn_w_down):
    pad = HEAD_DIM - 2 * HEADS
    w_gdn = jnp.pad(gdn_w_in, ((0, 0), (0, 0), (0, pad))).astype(BF16)
    gate_pad = lambda a: jnp.pad(a, ((0, 0), (HEADS, HEAD_DIM - 2 * HEADS)))[:, None, :]
    f = ffn_w_down.shape[1]
    return dict(
        w_gdn=w_gdn, alog=gate_pad(gdn_a_log), dtb=gate_pad(gdn_dt_bias),
        w_hgrn=hgrn_w_in.astype(BF16), gdn_w_out=gdn_w_out.astype(BF16), hgrn_w_out=hgrn_w_out.astype(BF16),
        wg=ffn_w_gu[:, :, :f].astype(BF16), wu=ffn_w_gu[:, :, f:].astype(BF16), wd=ffn_w_down.astype(BF16))


def _trunk(x, mod, s_gdn, s_gconv, s_hgrn, s_fconv, wts, p, nb, lb, nsq, lc):
    depth = mod.shape[0]
    bsz, seq, _ = x.shape
    qk = HEADS * HEAD_DIM
    if nsq == 1:
        chunks_avail = seq // CHUNK
    else:
        state_bytes = 5 * nsq * HEADS * HEAD_DIM * HEAD_DIM * 4
        chunks_avail = max(1, min(bsz // nsq, STATE_VMEM_BUDGET // state_bytes))
    gdn_nch = min(GDN_CHUNKS, chunks_avail)
    hgrn_nch = min(HGRN_CHUNKS, chunks_avail)
    new_gdn, new_gconv, new_hgrn, new_fconv = [], [], [], []
    for layer in range(depth):
        modl = mod[layer][:, None, :]
        j = layer // N_MIXERS
        if layer % N_MIXERS == 0:
            qkv, gt, ba = _inproj_call(x, modl, wts["norm_pre_mix"][layer], p["w_gdn"][j],
                                       (3 * qk, qk, HEAD_DIM), nb, lb)
            cbuf = jnp.pad(s_gconv[j], ((0, 0), (PAD_ROWS - (GDN_CONV - 1), 0), (0, 0)))
            og, s_new, cb_new = _gdn_call(qkv, gt, ba, s_gdn[j], cbuf, wts["gdn_conv_w"][j], wts["gdn_conv_b"][j],
                                          p["alog"][j], p["dtb"][j], wts["gdn_norm"][j], gdn_nch, nsq, lc)
            new_gdn.append(s_new)
            new_gconv.append(cb_new[:, PAD_ROWS - (GDN_CONV - 1):, :])
            w_out = p["gdn_w_out"][j]
        else:
            (proj,) = _inproj_call(x, modl, wts["norm_pre_mix"][layer], p["w_hgrn"][j], (4 * qk,), nb, lb)
            og, s_new = _hgrn_call(proj, s_hgrn[j], wts["hgrn_lb"], wts["hgrn_norm"][j], j, hgrn_nch, nsq, lc)
            new_hgrn.append(s_new)
            w_out = p["hgrn_w_out"][j]
        fbuf = jnp.pad(s_fconv[layer], ((0, 0), (PAD_ROWS - (FFN_CONV - 1), 0), (0, 0)))
        x, fb_new = _ffn_call(og, x, modl, wts["norm_post_mix"][layer], wts["norm_pre_ffn"][layer],
                              wts["norm_post_ffn"][layer], w_out, p["wg"][layer], p["wu"][layer],
                              wts["ffn_conv_w"][layer], wts["ffn_conv_b"][layer], p["wd"][layer], fbuf, nb, lb)
        new_fconv.append(fb_new[:, PAD_ROWS - (FFN_CONV - 1):, :])
    return x, jnp.stack(new_gdn), jnp.stack(new_gconv), jnp.stack(new_hgrn), jnp.stack(new_fconv)


def kernel(x_prompt, x_sample, state_gdn, state_gdn_conv, state_hgrn, state_ffn_conv, c_prompt, c_sample, ada_w, ada_b, norm_pre_mix, norm_post_mix, norm_pre_ffn, norm_post_ffn, gdn_w_in, gdn_conv_w, gdn_conv_b, gdn_a_log, gdn_dt_bias, gdn_norm, gdn_w_out, hgrn_lb, hgrn_w_in, hgrn_norm, hgrn_w_out, ffn_w_gu, ffn_conv_w, ffn_conv_b, ffn_w_down):
    wts = dict(norm_pre_mix=norm_pre_mix, norm_post_mix=norm_post_mix, norm_pre_ffn=norm_pre_ffn,
               norm_post_ffn=norm_post_ffn, gdn_conv_w=gdn_conv_w, gdn_conv_b=gdn_conv_b, gdn_norm=gdn_norm,
               hgrn_lb=hgrn_lb, hgrn_norm=hgrn_norm, ffn_conv_w=ffn_conv_w, ffn_conv_b=ffn_conv_b)
    p = _prep_weights(gdn_w_in, gdn_a_log, gdn_dt_bias, hgrn_w_in, gdn_w_out, hgrn_w_out, ffn_w_gu, ffn_w_down)

    bp, sp, _ = x_prompt.shape
    bs, ss, _ = x_sample.shape
    mod = _mod_call(jnp.concatenate([c_prompt, c_sample], axis=0), ada_w, ada_b)
    mod_p, mod_s = mod[:, :bp], mod[:, bp:]

    dt = x_prompt.dtype
    zeros = lambda a: jnp.zeros((a.shape[0], bp) + a.shape[2:], dt)
    y_p, p_gdn, p_gconv, p_hgrn, p_fconv = _trunk(
        x_prompt, mod_p, zeros(state_gdn), zeros(state_gdn_conv), zeros(state_hgrn), zeros(state_ffn_conv),
        wts, p, nb=1, lb=min(TILE_ROWS, sp), nsq=1, lc=min(CHUNK, sp))
    y_s, s_gdn, s_gconv, s_hgrn, s_fconv = _trunk(
        x_sample, mod_s, state_gdn, state_gdn_conv, state_hgrn, state_ffn_conv,
        wts, p, nb=min(bs, SHORT_TILE_ROWS // ss), lb=ss, nsq=min(bs, CHUNK // ss), lc=ss)
    return (y_p, y_s, p_gdn, p_gconv, p_hgrn, p_fconv, s_gdn, s_gconv, s_hgrn, s_fconv)
```

```python
import functools

import numpy as np
import jax
import jax.numpy as jnp
from jax import lax
from jax.experimental import pallas as pl
from jax.experimental.pallas import tpu as pltpu

F32 = jnp.float32
BF16 = jnp.bfloat16
EPS = 1e-6
LB_FLOOR = 1e-30
N_MOD = 6
N_MIXERS = 2
HEADS = 8
HEAD_DIM = 128
GDN_CONV = 4
FFN_CONV = 3
CHUNK = 64
PAD_ROWS = 8
TILE_ROWS = 512
SHORT_TILE_ROWS = 256
GDN_CHUNKS = 4
HGRN_CHUNKS = 4
ROW_PARTS = 2
VMEM_LIMIT = 56 * 1024 * 1024
STATE_VMEM_BUDGET = 32 * 1024 * 1024


def _params(*sem):
    return pltpu.CompilerParams(dimension_semantics=sem, vmem_limit_bytes=VMEM_LIMIT)


def _sigmoid(x):
    return jax.nn.sigmoid(x)


def _silu(x):
    return x * jax.nn.sigmoid(x)


def _softplus(x):
    return jnp.maximum(x, 0.0) + jnp.log1p(jnp.exp(-jnp.abs(x)))


def _rms(x, g):
    return x * lax.rsqrt(jnp.mean(x * x, axis=-1, keepdims=True) + EPS) * g


def _dot(a, b):
    return jnp.dot(a.astype(BF16), b.astype(BF16), preferred_element_type=F32)


def _bmm(a, b):
    return lax.dot_general(a.astype(BF16), b.astype(BF16), (((2,), (1,)), ((0,), (0,))),
                           preferred_element_type=F32)


def _bmm_nt(a, b):
    return lax.dot_general(a.astype(BF16), b.astype(BF16), (((2,), (2,)), ((0,), (0,))),
                           preferred_element_type=F32)


def _split2(x):
    hi = x.astype(BF16)
    lo = (x - hi.astype(F32)).astype(BF16)
    return hi, lo


def _bmm_hi(a, b):
    ah, al = _split2(a)
    bh, bl = _split2(b)
    return _bmm(ah, bh) + (_bmm(ah, bl) + _bmm(al, bh))


def _dot_sel(sel, x, pieces):
    acc = None
    r = x
    for i in range(pieces):
        part = r.astype(BF16)
        if i + 1 < pieces:
            r = r - part.astype(F32)
        term = jnp.dot(sel, part, preferred_element_type=F32)
        acc = term if acc is None else acc + term
    return acc


def _idiv(x, p2):
    return x >> (p2.bit_length() - 1)


def _unit_lower_inverse(a, block, span):
    n = a.shape[-1]
    row = lax.broadcasted_iota(jnp.int32, (n, n), 0)
    col = lax.broadcasted_iota(jnp.int32, (n, n), 1)
    eye = (row == col).astype(F32)[None]
    if block >= span:
        return _neumann(eye, a, span)
    diag = (_idiv(row, block) == _idiv(col, block))[None]
    inv = _neumann(eye, jnp.where(diag, a, 0.0), block)
    blk = _neumann(eye, _bmm_hi(inv, jnp.where(diag, 0.0, a)), span // block)
    return _bmm_hi(blk, inv)


def _neumann(eye, d, order):
    p = eye - d
    dp = d
    k = 2
    while k < order:
        dp = _bmm_hi(dp, dp)
        p = p + _bmm_hi(p, dp)
        k *= 2
    return p


def _items(x, nch, base, width):
    return jnp.concatenate(
        [x[ch * CHUNK:(ch + 1) * CHUNK, base + h * width:base + (h + 1) * width][None]
         for ch in range(nch) for h in range(HEADS)], axis=0)


def _mod_kernel(c_ref, w_ref, b_ref, o_ref):
    cs = _silu(c_ref[...])
    o_ref[0] = _dot(cs, w_ref[0]) + b_ref[0]


def _mod_call(c_all, ada_w, ada_b):
    depth, d, n = ada_w.shape
    rows = c_all.shape[0]
    tn = n // 4
    return pl.pallas_call(
        _mod_kernel,
        grid=(depth, n // tn),
        in_specs=[pl.BlockSpec((rows, d), lambda l, j: (0, 0)),
                  pl.BlockSpec((1, d, tn), lambda l, j: (l, 0, j)),
                  pl.BlockSpec((1, 1, tn), lambda l, j: (l, 0, j))],
        out_specs=pl.BlockSpec((1, rows, tn), lambda l, j: (l, 0, j)),
        out_shape=jax.ShapeDtypeStruct((depth, rows, n), F32),
        compiler_params=_params("arbitrary", "arbitrary"),
        name="adaln_mod",
    )(c_all, ada_w, ada_b.reshape(depth, 1, n))


def _row_parts(nb, lb, count):
    if nb >= count:
        return [(slice(i * nb // count, (i + 1) * nb // count), slice(0, lb)) for i in range(count)]
    return [(slice(0, nb), slice(i * lb // count, (i + 1) * lb // count)) for i in range(count)]


def _flat(v):
    return v.reshape(v.shape[0] * v.shape[1], v.shape[2])


def _inproj_kernel(x_ref, sh_ref, sc_ref, g_ref, w_ref, *out_refs, splits):
    nb, lb, d = x_ref.shape
    parts = _row_parts(nb, lb, ROW_PARTS)
    hbs = [_flat(_rms(x_ref[bs, rs, :], g_ref[...]) * (1.0 + sc_ref[bs]) + sh_ref[bs]).astype(BF16)
           for bs, rs in parts]
    off = 0
    for o_ref, n in zip(out_refs, splits):
        for (bs, rs), hb in zip(parts, hbs):
            res = jnp.dot(hb, w_ref[:, off:off + n], preferred_element_type=F32)
            o_ref[bs, rs, :] = res.reshape(bs.stop - bs.start, rs.stop - rs.start, n)
        off += n


def _inproj_gdn_kernel(x_ref, sh_ref, sc_ref, g_ref, w_ref, cw_ref, cb_ref, cbuf_ref,
                       qkv_ref, gt_ref, ba_ref, cbnew_ref, xbuf):
    nb, lb, d = x_ref.shape
    hd = HEAD_DIM
    qk = HEADS * hd
    w3 = 3 * qk
    t = pl.program_id(1)
    parts = _row_parts(nb, lb, ROW_PARTS)
    shape = lambda bs, rs: (bs.stop - bs.start, rs.stop - rs.start)

    @pl.when(t == 0)
    def _():
        xbuf[:, 0:PAD_ROWS, :] = cbuf_ref[...]

    @pl.when(t > 0)
    def _():
        xbuf[:, 0:PAD_ROWS, :] = xbuf[:, lb:lb + PAD_ROWS, :]

    hbs = [_flat(_rms(x_ref[bs, rs, :], g_ref[...]) * (1.0 + sc_ref[bs]) + sh_ref[bs]).astype(BF16)
           for bs, rs in parts]
    raws = [jnp.dot(hb, w_ref[:, 0:w3], preferred_element_type=F32) for hb in hbs]
    for (bs, rs), raw in zip(parts, raws):
        xbuf[bs, PAD_ROWS + rs.start:PAD_ROWS + rs.stop, :] = raw.reshape(shape(bs, rs) + (w3,))
    cbnew_ref[...] = xbuf[:, lb:lb + PAD_ROWS, :]

    cw = cw_ref[...]
    for (bs, rs), hb, raw in zip(parts, hbs, raws):
        pb, pr = shape(bs, rs)
        y = cb_ref[...] + raw.reshape(pb, pr, w3) * cw[GDN_CONV - 1:GDN_CONV]
        for j in range(GDN_CONV - 1):
            s = PAD_ROWS - (GDN_CONV - 1) + j
            y = y + xbuf[bs, s + rs.start:s + rs.stop, :] * cw[j:j + 1]
        a = _silu(y)
        for h in range(2 * HEADS):
            hs = slice(h * hd, (h + 1) * hd)
            v = a[:, :, hs]
            scale = lax.rsqrt(jnp.sum(v * v, axis=-1, keepdims=True) + EPS)
            qkv_ref[bs, rs, hs] = v * (scale * (hd ** -0.5) if h < HEADS else scale)
        qkv_ref[bs, rs, 2 * qk:w3] = a[:, :, 2 * qk:w3]
        gt_ref[bs, rs, :] = jnp.dot(hb, w_ref[:, w3:w3 + qk], preferred_element_type=F32).reshape(pb, pr, qk)
        ba_ref[bs, rs, :] = jnp.dot(hb, w_ref[:, w3 + qk:], preferred_element_type=F32).reshape(pb, pr, hd)


def _inproj_gdn_call(x, mod, gain, w, cw, cb, cbuf, nb, lb):
    bsz, seq, d = x.shape
    ntot = w.shape[1]
    qk = HEADS * HEAD_DIM
    w3 = 3 * qk
    row = lambda k: pl.BlockSpec((nb, 1, d), lambda b, t, k=k: (b, 0, k))
    tok = lambda n: pl.BlockSpec((nb, lb, n), lambda b, t: (b, t, 0))
    hist = pl.BlockSpec((nb, PAD_ROWS, w3), lambda b, t: (b, 0, 0))
    return pl.pallas_call(
        _inproj_gdn_kernel,
        grid=(bsz // nb, seq // lb),
        in_specs=[tok(d), row(0), row(1),
                  pl.BlockSpec((1, d), lambda b, t: (0, 0)),
                  pl.BlockSpec((d, ntot), lambda b, t: (0, 0), pipeline_mode=pl.Buffered(1)),
                  pl.BlockSpec((GDN_CONV, w3), lambda b, t: (0, 0)),
                  pl.BlockSpec((1, w3), lambda b, t: (0, 0)), hist],
        out_specs=[tok(w3), tok(qk), tok(HEAD_DIM), hist],
        out_shape=[jax.ShapeDtypeStruct((bsz, seq, w3), F32), jax.ShapeDtypeStruct((bsz, seq, qk), F32),
                   jax.ShapeDtypeStruct((bsz, seq, HEAD_DIM), F32),
                   jax.ShapeDtypeStruct((bsz, PAD_ROWS, w3), F32)],
        scratch_shapes=[pltpu.VMEM((nb, lb + PAD_ROWS, w3), F32)],
        compiler_params=_params("arbitrary", "arbitrary"),
        name="inproj_gdn",
    )(x, mod, mod, gain.reshape(1, d), w, cw, cb.reshape(1, w3), cbuf)


def _inproj_call(x, mod, gain, w, splits, nb, lb):
    bsz, seq, d = x.shape
    ntot = w.shape[1]
    row = lambda k: pl.BlockSpec((nb, 1, d), lambda b, t, k=k: (b, 0, k))
    return pl.pallas_call(
        functools.partial(_inproj_kernel, splits=splits),
        grid=(bsz // nb, seq // lb),
        in_specs=[pl.BlockSpec((nb, lb, d), lambda b, t: (b, t, 0)),
                  row(0), row(1),
                  pl.BlockSpec((1, d), lambda b, t: (0, 0)),
                  pl.BlockSpec((d, ntot), lambda b, t: (0, 0), pipeline_mode=pl.Buffered(1))],
        out_specs=[pl.BlockSpec((nb, lb, n), lambda b, t: (b, t, 0)) for n in splits],
        out_shape=[jax.ShapeDtypeStruct((bsz, seq, n), F32) for n in splits],
        compiler_params=_params("arbitrary", "arbitrary"),
        name="inproj",
    )(x, mod, mod, gain.reshape(1, d), w)


def _ffn_kernel(og_ref, x_ref, g1_ref, sh2_ref, sc2_ref, g2_ref, npost_ref, npre_ref, npostf_ref,
                wout_ref, wg_ref, wu_ref, cw_ref, cb_ref, wd_ref, fbuf_ref,
                xo_ref, fbnew_ref, gbuf):
    nb, lb, d = x_ref.shape
    f = wg_ref.shape[1]
    t = pl.program_id(1)
    parts = _row_parts(nb, lb, ROW_PARTS)
    shape = lambda bs, rs: (bs.stop - bs.start, rs.stop - rs.start)
    flat = _flat

    @pl.when(t == 0)
    def _():
        gbuf[:, 0:PAD_ROWS, :] = fbuf_ref[...]

    @pl.when(t > 0)
    def _():
        gbuf[:, 0:PAD_ROWS, :] = gbuf[:, lb:lb + PAD_ROWS, :]

    mix = [jnp.dot(flat(og_ref[bs, rs, :]).astype(BF16), wout_ref[...], preferred_element_type=F32)
           for bs, rs in parts]
    x1, hb = [], []
    for (bs, rs), mx in zip(parts, mix):
        pb, pr = shape(bs, rs)
        xa = x_ref[bs, rs, :] + (1.0 + g1_ref[bs]) * _rms(mx.reshape(pb, pr, d), npost_ref[...])
        x1.append(xa)
        hb.append(flat(_rms(xa, npre_ref[...]) * (1.0 + sc2_ref[bs]) + sh2_ref[bs]).astype(BF16))
    gt = [jnp.dot(h, wg_ref[...], preferred_element_type=F32) for h in hb]
    up = [jnp.dot(h, wu_ref[...], preferred_element_type=F32) for h in hb]
    for (bs, rs), g in zip(parts, gt):
        pb, pr = shape(bs, rs)
        gbuf[bs, PAD_ROWS + rs.start:PAD_ROWS + rs.stop, :] = g.reshape(pb, pr, f)
    fbnew_ref[...] = gbuf[:, lb:lb + PAD_ROWS, :]

    cw = cw_ref[...]
    out = []
    for (bs, rs), g, u in zip(parts, gt, up):
        pb, pr = shape(bs, rs)
        y = cb_ref[...] + g.reshape(pb, pr, f) * cw[FFN_CONV - 1:FFN_CONV]
        for j in range(FFN_CONV - 1):
            s = PAD_ROWS - (FFN_CONV - 1) + j
            y = y + gbuf[bs, s + rs.start:s + rs.stop, :] * cw[j:j + 1]
        act = flat(_silu(y)) * u
        out.append(jnp.dot(act.astype(BF16), wd_ref[...], preferred_element_type=F32))
    for (bs, rs), xa, o in zip(parts, x1, out):
        pb, pr = shape(bs, rs)
        xo_ref[bs, rs, :] = xa + (1.0 + g2_ref[bs]) * _rms(o.reshape(pb, pr, d), npostf_ref[...])


def _ffn_call(og, x, mod, npost, npre, npostf, wout, wg, wu, cw, cb, wd, fbuf, nb, lb):
    bsz, seq, d = x.shape
    f = wg.shape[1]
    row = lambda k: pl.BlockSpec((nb, 1, d), lambda b, t, k=k: (b, 0, k))
    vec = lambda n: pl.BlockSpec((1, n), lambda b, t: (0, 0))
    const = lambda r, c: pl.BlockSpec((r, c), lambda b, t: (0, 0), pipeline_mode=pl.Buffered(1))
    tok = pl.BlockSpec((nb, lb, d), lambda b, t: (b, t, 0))
    hist = pl.BlockSpec((nb, PAD_ROWS, f), lambda b, t: (b, 0, 0))
    return pl.pallas_call(
        _ffn_kernel,
        grid=(bsz // nb, seq // lb),
        in_specs=[tok, tok, row(2), row(3), row(4), row(5), vec(d), vec(d), vec(d),
                  const(d, d), const(d, f), const(d, f),
                  pl.BlockSpec((FFN_CONV, f), lambda b, t: (0, 0)), vec(f), const(f, d), hist],
        out_specs=[tok, hist],
        out_shape=[jax.ShapeDtypeStruct((bsz, seq, d), F32),
                   jax.ShapeDtypeStruct((bsz, PAD_ROWS, f), F32)],
        scratch_shapes=[pltpu.VMEM((nb, lb + PAD_ROWS, f), F32)],
        compiler_params=_params("arbitrary", "arbitrary"),
        name="ffn_block",
    )(og, x, mod, mod, mod, mod, npost.reshape(1, d), npre.reshape(1, d), npostf.reshape(1, d),
      wout, wg, wu, cw, cb.reshape(1, f), wd, fbuf)


def _seq_masks(nsq, lc):
    c = nsq * lc
    i = np.arange(c)[:, None]
    j = np.arange(c)[None, :]
    same = (i // lc) == (j // lc)
    return i, j, same


def _gdn_consts(nsq, lc):
    i, j, same = _seq_masks(nsq, lc)
    return jnp.asarray((same & (j <= i)).astype(np.float32), BF16)


def _hgrn_levels(lc):
    s, out = lc, []
    while s >= 2:
        out.append(s)
        s //= 2
    return out


def _hgrn_consts(nsq, lc):
    i, j, same = _seq_masks(nsq, lc)
    mats = [same & (j <= i), same & (j > i)]
    for s in _hgrn_levels(lc):
        mid = (i // s) * s + s // 2
        mats.append(np.where(i >= mid, (j >= mid) & (j <= i), (j > i) & (j < mid)))
    return jnp.asarray(np.concatenate(mats, axis=0).astype(np.float32), BF16)


def _state_index(ch, n, nsq, lc, lr):
    return ((ch * nsq + n) * lc) // lr


def _store_heads(og_ref, ch, nsq, lc, val):
    for h in range(HEADS):
        hs = slice(h * HEAD_DIM, (h + 1) * HEAD_DIM)
        if og_ref.shape[0] == 1:
            og_ref[0, ch * CHUNK:(ch + 1) * CHUNK, hs] = val[h]
        else:
            og_ref[ch * nsq:(ch + 1) * nsq, :, hs] = val[h].reshape(nsq, lc, HEAD_DIM)


def _gdn_kernel(qkv_ref, gt_ref, ba_ref, s0_ref, alog_ref, dtb_ref, ng_ref, tri_ref, *rest, nch, nsq, lc):
    og_ref, snew_ref, s_scr = rest[-3:]
    ns, lr, w3 = qkv_ref.shape
    r = ns * lr
    hd = HEAD_DIM
    qk = HEADS * hd
    t = pl.program_id(1)
    last = pl.num_programs(1) - 1

    @pl.when(t == 0)
    def _():
        s_scr[...] = s0_ref[...]

    qkv = qkv_ref[...].reshape(r, w3)
    gt = gt_ref[...].reshape(r, qk)

    ba = ba_ref[...].reshape(r, hd)
    beta = _sigmoid(ba)
    g = -jnp.exp(alog_ref[...]) * _softplus(ba + dtb_ref[...])
    chunks = [slice(ch * CHUNK, (ch + 1) * CHUNK) for ch in range(nch)]
    gcum = [_dot_sel(tri_ref[...], g[sl], 3) for sl in chunks]
    gcum_t = [x.T for x in gcum]
    per_item = lambda f: jnp.concatenate([f(ch, h)[None] for ch in range(nch) for h in range(HEADS)], axis=0)
    gc = per_item(lambda ch, h: gcum[ch][:, HEADS + h:HEADS + h + 1])
    gr = per_item(lambda ch, h: gcum_t[ch][HEADS + h:HEADS + h + 1, :])
    bcol = per_item(lambda ch, h: beta[chunks[ch], h:h + 1])

    q = _items(qkv, nch, 0, hd)
    k = _items(qkv, nch, qk, hd)
    v = _items(qkv, nch, 2 * qk, hd)

    row = lax.broadcasted_iota(jnp.int32, (CHUNK, CHUNK), 0)
    col = lax.broadcasted_iota(jnp.int32, (CHUNK, CHUNK), 1)
    same = _idiv(row, lc) == _idiv(col, lc)
    incl = (same & (col <= row))[None]
    strict = (same & (col < row))[None]
    rid = _idiv(lax.broadcasted_iota(jnp.int32, (1, CHUNK, 1), 1), lc)

    decay = jnp.where(incl, jnp.exp(jnp.where(incl, gc - gr, 0.0)), 0.0)
    eg = jnp.exp(gc)
    kb = k * bcol
    a = jnp.where(strict, _bmm_nt(kb, k) * decay, 0.0)
    tinv = _unit_lower_inverse(a, min(16, lc), lc)
    rhs = jnp.concatenate([v * bcol, kb * eg], axis=-1).astype(BF16)
    th, tl = _split2(tinv)
    sol = _bmm(th, rhs) + _bmm(tl, rhs)
    uv, w = sol[..., :hd], sol[..., hd:]
    attn = jnp.where(incl, _bmm_nt(q, k) * decay, 0.0)
    qg = q * eg
    gate = _silu(_items(gt, nch, 0, hd))

    for ch in range(nch):
        it = slice(ch * HEADS, (ch + 1) * HEADS)
        us, os_ = [], []
        for n in range(nsq):
            sl = slice(n * lc, (n + 1) * lc)
            st = s_scr[_state_index(ch, n, nsq, lc, lr)]
            ws = _bmm(jnp.concatenate([w[it, sl], qg[it, sl]], axis=1), st)
            us.append(uv[it, sl] - ws[:, :lc])
            os_.append(ws[:, lc:])
        u = us[0] if nsq == 1 else jnp.concatenate(us, axis=1)
        o = (os_[0] if nsq == 1 else jnp.concatenate(os_, axis=1)) + _bmm(attn[it], u)
        for n in range(nsq):
            si = _state_index(ch, n, nsq, lc, lr)
            gl = gc[it, (n + 1) * lc - 1:(n + 1) * lc, :]
            if nsq > 1:
                kd = jnp.where(rid == n, k[it] * jnp.exp(jnp.where(rid == n, gl - gc[it], 0.0)), 0.0)
            else:
                kd = k[it] * jnp.exp(gl - gc[it])
            s_scr[si] = s_scr[si] * jnp.exp(gl) + _bmm(jnp.swapaxes(kd, 1, 2), u)
        _store_heads(og_ref, ch, nsq, lc, _rms(o, ng_ref[...]) * gate[it])

    @pl.when(t == last)
    def _():
        for n in range(ns):
            _write_state(snew_ref, n, s_scr[n])


def _stacked_state(states, j, prev, ns):
    tail = states.shape[2:]
    in_spec = pl.BlockSpec((None, ns) + tail, lambda b, t: (j, b, 0, 0, 0))
    if prev is None:
        out_spec = pl.BlockSpec((states.shape[0], ns) + tail, lambda b, t: (0, b, 0, 0, 0))
        extra_in, extra_specs = [], []
    else:
        out_spec = in_spec
        extra_in, extra_specs = [prev], [pl.BlockSpec(memory_space=pl.ANY)]
    return in_spec, out_spec, extra_in, extra_specs, jax.ShapeDtypeStruct(states.shape, states.dtype)


def _write_state(snew_ref, n, value):
    if len(snew_ref.shape) == 5:
        for m in range(snew_ref.shape[0]):
            snew_ref[m, n] = value
    else:
        snew_ref[n] = value


def _gdn_call(qkv, gt, ba, states, j, prev, alog, dtb, ng, nch, nsq, lc):
    bsz, seq, w3 = qkv.shape
    qk = w3 // 3
    hd = HEAD_DIM
    tri = _gdn_consts(nsq, lc)
    ns, lr = (1, nch * CHUNK) if nsq == 1 else (nch * nsq, lc)
    tok = lambda n: pl.BlockSpec((ns, lr, n), lambda b, t: (b, t, 0))
    vec = lambda n: pl.BlockSpec((1, n), lambda b, t: (0, 0))
    state, state_out, extra_in, extra_specs, state_shape = _stacked_state(states, j, prev, ns)
    inputs = [qkv, gt, ba, states, alog, dtb, ng.reshape(1, hd), tri] + extra_in
    return pl.pallas_call(
        functools.partial(_gdn_kernel, nch=nch, nsq=nsq, lc=lc),
        grid=(bsz // ns, seq // lr),
        in_specs=[tok(w3), tok(qk), tok(hd), state, vec(hd), vec(hd), vec(hd),
                  pl.BlockSpec((CHUNK, CHUNK), lambda b, t: (0, 0))] + extra_specs,
        out_specs=[tok(qk), state_out],
        out_shape=[jax.ShapeDtypeStruct((bsz, seq, qk), F32), state_shape],
        input_output_aliases={len(inputs) - 1: 1} if extra_in else {},
        scratch_shapes=[pltpu.VMEM((ns, HEADS, hd, hd), F32)],
        compiler_params=_params("arbitrary", "arbitrary"),
        name="gdn_core",
    )(*inputs)


def _hgrn_kernel(proj_ref, s0_ref, lbraw_ref, ng_ref, cm_ref, *rest, nch, nsq, lc, layer):
    og_ref, snew_ref, st_scr = rest[-3:]
    ns, lr, w4 = proj_ref.shape
    r = ns * lr
    hd = HEAD_DIM
    qf = HEADS * hd
    t = pl.program_id(1)
    last = pl.num_programs(1) - 1

    @pl.when(t == 0)
    def _():
        for n in range(ns):
            st_scr[n] = jnp.swapaxes(s0_ref[n], 1, 2)

    lbraw = lbraw_ref[...]
    e = jnp.exp(lbraw - jnp.max(lbraw, axis=0, keepdims=True))
    p = e / jnp.sum(e, axis=0, keepdims=True)
    lb = jnp.zeros((1, qf), F32)
    for m in range(1, layer + 1):
        lb = lb + p[m:m + 1, :]

    proj = proj_ref[...].reshape(r, w4)
    fr = proj[:, qf:2 * qf]
    log_sig = jnp.minimum(fr, 0.0) - jnp.log1p(jnp.exp(-jnp.abs(fr)))
    t1 = jnp.log(jnp.maximum(lb, LB_FLOOR))
    t2 = jnp.log1p(-lb) + log_sig
    logf = jnp.maximum(t1, t2) + jnp.log1p(jnp.exp(-jnp.abs(t1 - t2)))

    levels = _hgrn_levels(lc)
    nmat = 2 + len(levels)
    cm = cm_ref[...]
    ex = jnp.concatenate([jnp.exp(_dot_sel(cm, logf[ch * CHUNK:(ch + 1) * CHUNK], 2)) for ch in range(nch)], axis=0)
    factor = lambda m: jnp.concatenate(
        [ex[(ch * nmat + m) * CHUNK:(ch * nmat + m + 1) * CHUNK, h * hd:(h + 1) * hd][None]
         for ch in range(nch) for h in range(HEADS)], axis=0)

    q = _silu(_items(proj, nch, 0, hd))
    k = _items((1.0 - lb) * _sigmoid(-fr), nch, 0, hd)
    v = _items(proj, nch, 2 * qf, hd)
    gate = _silu(_items(proj, nch, 3 * qf, hd))

    row = lax.broadcasted_iota(jnp.int32, (CHUNK, CHUNK), 0)
    col = lax.broadcasted_iota(jnp.int32, (CHUNK, CHUNK), 1)
    attn = jnp.where((row == col)[None], _bmm_nt(q, k), 0.0)
    for li, s in enumerate(levels):
        mask = (_idiv(row, s) == _idiv(col, s)) & ((row & (s - 1)) >= s // 2) & ((col & (s - 1)) < s // 2)
        fl = factor(2 + li)
        attn = attn + jnp.where(mask[None], _bmm_nt(q * fl, k * fl), 0.0)
    eg = factor(0)
    qe = q * eg
    kd = k * factor(1)
    cid = _idiv(lax.broadcasted_iota(jnp.int32, (1, 1, CHUNK), 2), lc)

    for ch in range(nch):
        it = slice(ch * HEADS, (ch + 1) * HEADS)
        vt = jnp.swapaxes(v[it], 1, 2)
        os_ = []
        for n in range(nsq):
            sl = slice(n * lc, (n + 1) * lc)
            si = _state_index(ch, n, nsq, lc, lr)
            st = st_scr[si]
            os_.append(_bmm_nt(qe[it, sl], st))
            vtn = vt if nsq == 1 else jnp.where(cid == n, vt, 0.0)
            egl = eg[it, (n + 1) * lc - 1:(n + 1) * lc, :]
            st_scr[si] = st * egl + _bmm(vtn, kd[it])
        o = (os_[0] if nsq == 1 else jnp.concatenate(os_, axis=1)) + _bmm(attn[it], v[it])
        _store_heads(og_ref, ch, nsq, lc, _rms(o, ng_ref[...]) * gate[it])

    @pl.when(t == last)
    def _():
        for n in range(ns):
            _write_state(snew_ref, n, jnp.swapaxes(st_scr[n], 1, 2))


def _hgrn_call(proj, states, j, prev, lbraw, ng, nch, nsq, lc):
    bsz, seq, w4 = proj.shape
    qf = w4 // 4
    hd = HEAD_DIM
    cm = _hgrn_consts(nsq, lc)
    ns, lr = (1, nch * CHUNK) if nsq == 1 else (nch * nsq, lc)
    tok = lambda n: pl.BlockSpec((ns, lr, n), lambda b, t: (b, t, 0))
    state, state_out, extra_in, extra_specs, state_shape = _stacked_state(states, j, prev, ns)
    inputs = [proj, states, lbraw, ng.reshape(1, hd), cm] + extra_in
    return pl.pallas_call(
        functools.partial(_hgrn_kernel, nch=nch, nsq=nsq, lc=lc, layer=j),
        grid=(bsz // ns, seq // lr),
        in_specs=[tok(w4), state,
                  pl.BlockSpec(lbraw.shape, lambda b, t: (0, 0)),
                  pl.BlockSpec((1, hd), lambda b, t: (0, 0)),
                  pl.BlockSpec(cm.shape, lambda b, t: (0, 0))] + extra_specs,
        out_specs=[tok(qf), state_out],
        out_shape=[jax.ShapeDtypeStruct((bsz, seq, qf), F32), state_shape],
        input_output_aliases={len(inputs) - 1: 1} if extra_in else {},
        scratch_shapes=[pltpu.VMEM((ns, HEADS, hd, hd), F32)],
        compiler_params=_params("arbitrary", "arbitrary"),
        name="hgrn_core",
    )(*inputs)


def _prep_weights(gdn_w_in, gdn_a_log, gdn_dt_bias, hgrn_w_in, gdn_w_out, hgrn_w_out, ffn_w_gu, ffn_w_down):
    pad = HEAD_DIM - 2 * HEADS
    w_gdn = jnp.pad(gdn_w_in, ((0, 0), (0, 0), (0, pad))).astype(BF16)
    gate_pad = lambda a: jnp.pad(a, ((0, 0), (HEADS, HEAD_DIM - 2 * HEADS)))[:, None, :]
    f = ffn_w_down.shape[1]
    return dict(
        w_gdn=w_gdn, alog=gate_pad(gdn_a_log), dtb=gate_pad(gdn_dt_bias),
        w_hgrn=hgrn_w_in.astype(BF16), gdn_w_out=gdn_w_out.astype(BF16), hgrn_w_out=hgrn_w_out.astype(BF16),
        wg=ffn_w_gu[:, :, :f].astype(BF16), wu=ffn_w_gu[:, :, f:].astype(BF16), wd=ffn_w_down.astype(BF16))


def _trunk(x, mod, s_gdn, s_gconv, s_hgrn, s_fconv, wts, p, nb, lb, nsq, lc):
    depth = mod.shape[0]
    bsz, seq, _ = x.shape
    qk = HEADS * HEAD_DIM
    if nsq == 1:
        chunks_avail = seq // CHUNK
    else:
        state_bytes = (3 + 2 * max(s_gdn.shape[0], s_hgrn.shape[0])) * nsq * HEADS * HEAD_DIM * HEAD_DIM * 4
        chunks_avail = max(1, min(bsz // nsq, STATE_VMEM_BUDGET // state_bytes))
    gdn_nch = min(GDN_CHUNKS, chunks_avail)
    hgrn_nch = min(HGRN_CHUNKS, chunks_avail)
    new_gdn, new_gconv, new_hgrn, new_fconv = None, [], None, []
    for layer in range(depth):
        modl = mod[layer][:, None, :]
        j = layer // N_MIXERS
        if layer % N_MIXERS == 0:
            cbuf = jnp.pad(s_gconv[j], ((0, 0), (PAD_ROWS - (GDN_CONV - 1), 0), (0, 0)))
            qkv, gt, ba, cb_new = _inproj_gdn_call(x, modl, wts["norm_pre_mix"][layer], p["w_gdn"][j],
                                                   wts["gdn_conv_w"][j], wts["gdn_conv_b"][j], cbuf, nb, lb)
            og, new_gdn = _gdn_call(qkv, gt, ba, s_gdn, j, new_gdn, p["alog"][j], p["dtb"][j],
                                    wts["gdn_norm"][j], gdn_nch, nsq, lc)
            new_gconv.append(cb_new[:, PAD_ROWS - (GDN_CONV - 1):, :])
            w_out = p["gdn_w_out"][j]
        else:
            (proj,) = _inproj_call(x, modl, wts["norm_pre_mix"][layer], p["w_hgrn"][j], (4 * qk,), nb, lb)
            og, new_hgrn = _hgrn_call(proj, s_hgrn, j, new_hgrn, wts["hgrn_lb"], wts["hgrn_norm"][j],
                                      hgrn_nch, nsq, lc)
            w_out = p["hgrn_w_out"][j]
        fbuf = jnp.pad(s_fconv[layer], ((0, 0), (PAD_ROWS - (FFN_CONV - 1), 0), (0, 0)))
        x, fb_new = _ffn_call(og, x, modl, wts["norm_post_mix"][layer], wts["norm_pre_ffn"][layer],
                              wts["norm_post_ffn"][layer], w_out, p["wg"][layer], p["wu"][layer],
                              wts["ffn_conv_w"][layer], wts["ffn_conv_b"][layer], p["wd"][layer], fbuf, nb, lb)
        new_fconv.append(fb_new[:, PAD_ROWS - (FFN_CONV - 1):, :])
    return x, new_gdn, jnp.stack(new_gconv), new_hgrn, jnp.stack(new_fconv)


def kernel(x_prompt, x_sample, state_gdn, state_gdn_conv, state_hgrn, state_ffn_conv, c_prompt, c_sample, ada_w, ada_b, norm_pre_mix, norm_post_mix, norm_pre_ffn, norm_post_ffn, gdn_w_in, gdn_conv_w, gdn_conv_b, gdn_a_log, gdn_dt_bias, gdn_norm, gdn_w_out, hgrn_lb, hgrn_w_in, hgrn_norm, hgrn_w_out, ffn_w_gu, ffn_conv_w, ffn_conv_b, ffn_w_down):
    wts = dict(norm_pre_mix=norm_pre_mix, norm_post_mix=norm_post_mix, norm_pre_ffn=norm_pre_ffn,
               norm_post_ffn=norm_post_ffn, gdn_conv_w=gdn_conv_w, gdn_conv_b=gdn_conv_b, gdn_norm=gdn_norm,
               hgrn_lb=hgrn_lb, hgrn_norm=hgrn_norm, ffn_conv_w=ffn_conv_w, ffn_conv_b=ffn_conv_b)
    p = _prep_weights(gdn_w_in, gdn_a_log, gdn_dt_bias, hgrn_w_in, gdn_w_out, hgrn_w_out, ffn_w_gu, ffn_w_down)

    bp, sp, _ = x_prompt.shape
    bs, ss, _ = x_sample.shape
    mod = _mod_call(jnp.concatenate([c_prompt, c_sample], axis=0), ada_w, ada_b)
    mod_p, mod_s = mod[:, :bp], mod[:, bp:]

    dt = x_prompt.dtype
    zeros = lambda a: jnp.zeros((a.shape[0], bp) + a.shape[2:], dt)
    y_p, p_gdn, p_gconv, p_hgrn, p_fconv = _trunk(
        x_prompt, mod_p, zeros(state_gdn), zeros(state_gdn_conv), zeros(state_hgrn), zeros(state_ffn_conv),
        wts, p, nb=1, lb=min(TILE_ROWS, sp), nsq=1, lc=min(CHUNK, sp))
    y_s, s_gdn, s_gconv, s_hgrn, s_fconv = _trunk(
        x_sample, mod_s, state_gdn, state_gdn_conv, state_hgrn, state_ffn_conv,
        wts, p, nb=min(bs, SHORT_TILE_ROWS // ss), lb=ss, nsq=min(bs, CHUNK // ss), lc=ss)
    return (y_p, y_s, p_gdn, p_gconv, p_hgrn, p_fconv, s_gdn, s_gconv, s_hgrn, s_fconv)
```

```python
import functools

import numpy as np
import jax
import jax.numpy as jnp
from jax import lax
from jax.experimental import pallas as pl
from jax.experimental.pallas import tpu as pltpu

F32 = jnp.float32
BF16 = jnp.bfloat16
EPS = 1e-6
LB_FLOOR = 1e-30
N_MOD = 6
N_MIXERS = 2
HEADS = 8
HEAD_DIM = 128
GDN_CONV = 4
FFN_CONV = 3
CHUNK = 64
PAD_ROWS = 8
TILE_ROWS = 512
SHORT_TILE_ROWS = 256
GDN_CHUNKS = 4
HGRN_CHUNKS = 4
ROW_PARTS = 2
VMEM_LIMIT = 56 * 1024 * 1024
STATE_VMEM_BUDGET = 32 * 1024 * 1024


def _params(*sem):
    return pltpu.CompilerParams(dimension_semantics=sem, vmem_limit_bytes=VMEM_LIMIT)


def _sigmoid(x):
    return jax.nn.sigmoid(x)


def _silu(x):
    hx = 0.5 * x
    return hx + hx * jnp.tanh(hx)


def _softplus(x):
    return jnp.maximum(x, 0.0) + jnp.log1p(jnp.exp(-jnp.abs(x)))


def _rms(x, g):
    return x * lax.rsqrt(jnp.mean(x * x, axis=-1, keepdims=True) + EPS) * g


def _dot(a, b):
    return jnp.dot(a.astype(BF16), b.astype(BF16), preferred_element_type=F32)


def _bmm(a, b):
    return lax.dot_general(a.astype(BF16), b.astype(BF16), (((2,), (1,)), ((0,), (0,))),
                           preferred_element_type=F32)


def _bmm_nt(a, b):
    return lax.dot_general(a.astype(BF16), b.astype(BF16), (((2,), (2,)), ((0,), (0,))),
                           preferred_element_type=F32)


def _split2(x):
    hi = x.astype(BF16)
    lo = (x - hi.astype(F32)).astype(BF16)
    return hi, lo


def _pair_product(lhs, rhs, left):
    c = rhs.shape[1]
    n = len(lhs)
    rh, rl = _split2(rhs)
    bd = lambda x: jnp.concatenate([jnp.where(left, x, jnp.zeros_like(x)),
                                    jnp.where(left, jnp.zeros_like(x), x)], axis=1)
    pieces = [_split2(x) for x in lhs]
    his = [hi for hi, _ in pieces]
    los = [lo for _, lo in pieces]
    top = _bmm(jnp.concatenate(his + los, axis=1), bd(rh))
    bot = _bmm(his[0] if n == 1 else jnp.concatenate(his, axis=1), bd(rl))
    return [top[:, i * c:(i + 1) * c] + top[:, (n + i) * c:(n + i + 1) * c] + bot[:, i * c:(i + 1) * c]
            for i in range(n)]


def _dot_sel(sel, x, pieces):
    acc = None
    r = x
    for i in range(pieces):
        part = r.astype(BF16)
        if i + 1 < pieces:
            r = r - part.astype(F32)
        term = jnp.dot(sel, part, preferred_element_type=F32)
        acc = term if acc is None else acc + term
    return acc


def _idiv(x, p2):
    return x >> (p2.bit_length() - 1)


def _paired_unit_lower_inverse(a, block, span, left, row, col):
    eye = (col == row).astype(F32)
    if block >= span:
        return _paired_neumann(eye, a, span, left)
    diag = _idiv(row, block) == _idiv(col, block)
    inv = _paired_neumann(eye, jnp.where(diag, a, 0.0), block, left)
    (low,) = _pair_product([inv], jnp.where(diag, 0.0, a), left)
    blk = _paired_neumann(eye, low, span // block, left)
    (res,) = _pair_product([blk], inv, left)
    return res


def _paired_neumann(eye, d, order, left):
    p = eye - d
    if order <= 2:
        return p
    (dp,) = _pair_product([d], d, left)
    k = 4
    while k < order:
        pp, dp_next = _pair_product([p, dp], dp, left)
        p, dp = p + pp, dp_next
        k *= 2
    (pp,) = _pair_product([p], dp, left)
    return p + pp


def _items(x, nch, base, width):
    return jnp.concatenate(
        [x[ch * CHUNK:(ch + 1) * CHUNK, base + h * width:base + (h + 1) * width][None]
         for ch in range(nch) for h in range(HEADS)], axis=0)


def _mod_kernel(c_ref, w_ref, b_ref, o_ref):
    cs = _silu(c_ref[...])
    o_ref[0] = _dot(cs, w_ref[0]) + b_ref[0]


def _mod_call(c_all, ada_w, ada_b):
    depth, d, n = ada_w.shape
    rows = c_all.shape[0]
    tn = n // 4
    return pl.pallas_call(
        _mod_kernel,
        grid=(depth, n // tn),
        in_specs=[pl.BlockSpec((rows, d), lambda l, j: (0, 0)),
                  pl.BlockSpec((1, d, tn), lambda l, j: (l, 0, j)),
                  pl.BlockSpec((1, 1, tn), lambda l, j: (l, 0, j))],
        out_specs=pl.BlockSpec((1, rows, tn), lambda l, j: (l, 0, j)),
        out_shape=jax.ShapeDtypeStruct((depth, rows, n), F32),
        compiler_params=_params("arbitrary", "arbitrary"),
        name="adaln_mod",
    )(c_all, ada_w, ada_b.reshape(depth, 1, n))


def _row_parts(nb, lb, count):
    if nb >= count:
        return [(slice(i * nb // count, (i + 1) * nb // count), slice(0, lb)) for i in range(count)]
    return [(slice(0, nb), slice(i * lb // count, (i + 1) * lb // count)) for i in range(count)]


def _flat(v):
    return v.reshape(v.shape[0] * v.shape[1], v.shape[2])


def _inproj_kernel(x_ref, sh_ref, sc_ref, g_ref, w_ref, *out_refs, splits):
    nb, lb, d = x_ref.shape
    parts = _row_parts(nb, lb, ROW_PARTS)
    hbs = [_flat(_rms(x_ref[bs, rs, :], g_ref[...]) * (1.0 + sc_ref[bs]) + sh_ref[bs]).astype(BF16)
           for bs, rs in parts]
    off = 0
    for o_ref, n in zip(out_refs, splits):
        for (bs, rs), hb in zip(parts, hbs):
            res = jnp.dot(hb, w_ref[:, off:off + n], preferred_element_type=F32)
            o_ref[bs, rs, :] = res.reshape(bs.stop - bs.start, rs.stop - rs.start, n)
        off += n


def _inproj_gdn_kernel(x_ref, sh_ref, sc_ref, g_ref, w_ref, cw_ref, cb_ref, cbuf_ref,
                       qkv_ref, gt_ref, ba_ref, cbnew_ref, xbuf):
    nb, lb, d = x_ref.shape
    hd = HEAD_DIM
    qk = HEADS * hd
    w3 = 3 * qk
    t = pl.program_id(1)
    parts = _row_parts(nb, lb, ROW_PARTS)
    shape = lambda bs, rs: (bs.stop - bs.start, rs.stop - rs.start)

    @pl.when(t == 0)
    def _():
        xbuf[:, 0:PAD_ROWS, :] = cbuf_ref[...]

    @pl.when(t > 0)
    def _():
        xbuf[:, 0:PAD_ROWS, :] = xbuf[:, lb:lb + PAD_ROWS, :]

    hbs = [_flat(_rms(x_ref[bs, rs, :], g_ref[...]) * (1.0 + sc_ref[bs]) + sh_ref[bs]).astype(BF16)
           for bs, rs in parts]
    raws = [jnp.dot(hb, w_ref[:, 0:w3], preferred_element_type=F32) for hb in hbs]
    for (bs, rs), raw in zip(parts, raws):
        xbuf[bs, PAD_ROWS + rs.start:PAD_ROWS + rs.stop, :] = raw.reshape(shape(bs, rs) + (w3,))
    cbnew_ref[...] = xbuf[:, lb:lb + PAD_ROWS, :]

    cw = cw_ref[...]
    for (bs, rs), hb, raw in zip(parts, hbs, raws):
        pb, pr = shape(bs, rs)
        y = cb_ref[...] + raw.reshape(pb, pr, w3) * cw[GDN_CONV - 1:GDN_CONV]
        for j in range(GDN_CONV - 1):
            s = PAD_ROWS - (GDN_CONV - 1) + j
            y = y + xbuf[bs, s + rs.start:s + rs.stop, :] * cw[j:j + 1]
        a = _silu(y)
        for h in range(2 * HEADS):
            hs = slice(h * hd, (h + 1) * hd)
            v = a[:, :, hs]
            scale = lax.rsqrt(jnp.sum(v * v, axis=-1, keepdims=True) + EPS)
            qkv_ref[bs, rs, hs] = v * (scale * (hd ** -0.5) if h < HEADS else scale)
        qkv_ref[bs, rs, 2 * qk:w3] = a[:, :, 2 * qk:w3]
        gt_ref[bs, rs, :] = jnp.dot(hb, w_ref[:, w3:w3 + qk], preferred_element_type=F32).reshape(pb, pr, qk)
        ba_ref[bs, rs, :] = jnp.dot(hb, w_ref[:, w3 + qk:], preferred_element_type=F32).reshape(pb, pr, hd)


def _inproj_gdn_call(x, mod, gain, w, cw, cb, cbuf, nb, lb):
    bsz, seq, d = x.shape
    ntot = w.shape[1]
    qk = HEADS * HEAD_DIM
    w3 = 3 * qk
    row = lambda k: pl.BlockSpec((nb, 1, d), lambda b, t, k=k: (b, 0, k))
    tok = lambda n: pl.BlockSpec((nb, lb, n), lambda b, t: (b, t, 0))
    hist = pl.BlockSpec((nb, PAD_ROWS, w3), lambda b, t: (b, 0, 0))
    return pl.pallas_call(
        _inproj_gdn_kernel,
        grid=(bsz // nb, seq // lb),
        in_specs=[tok(d), row(0), row(1),
                  pl.BlockSpec((1, d), lambda b, t: (0, 0)),
                  pl.BlockSpec((d, ntot), lambda b, t: (0, 0), pipeline_mode=pl.Buffered(1)),
                  pl.BlockSpec((GDN_CONV, w3), lambda b, t: (0, 0)),
                  pl.BlockSpec((1, w3), lambda b, t: (0, 0)), hist],
        out_specs=[tok(w3), tok(qk), tok(HEAD_DIM), hist],
        out_shape=[jax.ShapeDtypeStruct((bsz, seq, w3), F32), jax.ShapeDtypeStruct((bsz, seq, qk), F32),
                   jax.ShapeDtypeStruct((bsz, seq, HEAD_DIM), F32),
                   jax.ShapeDtypeStruct((bsz, PAD_ROWS, w3), F32)],
        scratch_shapes=[pltpu.VMEM((nb, lb + PAD_ROWS, w3), F32)],
        compiler_params=_params("arbitrary", "arbitrary"),
        name="inproj_gdn",
    )(x, mod, mod, gain.reshape(1, d), w, cw, cb.reshape(1, w3), cbuf)


def _inproj_call(x, mod, gain, w, splits, nb, lb):
    bsz, seq, d = x.shape
    ntot = w.shape[1]
    row = lambda k: pl.BlockSpec((nb, 1, d), lambda b, t, k=k: (b, 0, k))
    return pl.pallas_call(
        functools.partial(_inproj_kernel, splits=splits),
        grid=(bsz // nb, seq // lb),
        in_specs=[pl.BlockSpec((nb, lb, d), lambda b, t: (b, t, 0)),
                  row(0), row(1),
                  pl.BlockSpec((1, d), lambda b, t: (0, 0)),
                  pl.BlockSpec((d, ntot), lambda b, t: (0, 0), pipeline_mode=pl.Buffered(1))],
        out_specs=[pl.BlockSpec((nb, lb, n), lambda b, t: (b, t, 0)) for n in splits],
        out_shape=[jax.ShapeDtypeStruct((bsz, seq, n), F32) for n in splits],
        compiler_params=_params("arbitrary", "arbitrary"),
        name="inproj",
    )(x, mod, mod, gain.reshape(1, d), w)


def _ffn_kernel(og_ref, x_ref, g1_ref, sh2_ref, sc2_ref, g2_ref, npost_ref, npre_ref, npostf_ref,
                wout_ref, wg_ref, wu_ref, cw_ref, cb_ref, wd_ref, fbuf_ref,
                xo_ref, fbnew_ref, gbuf):
    nb, lb, d = x_ref.shape
    f = wg_ref.shape[1]
    t = pl.program_id(1)
    parts = _row_parts(nb, lb, ROW_PARTS)
    shape = lambda bs, rs: (bs.stop - bs.start, rs.stop - rs.start)
    flat = _flat

    @pl.when(t == 0)
    def _():
        gbuf[:, 0:PAD_ROWS, :] = fbuf_ref[...]

    @pl.when(t > 0)
    def _():
        gbuf[:, 0:PAD_ROWS, :] = gbuf[:, lb:lb + PAD_ROWS, :]

    mix = [jnp.dot(flat(og_ref[bs, rs, :]).astype(BF16), wout_ref[...], preferred_element_type=F32)
           for bs, rs in parts]
    x1, hb = [], []
    for (bs, rs), mx in zip(parts, mix):
        pb, pr = shape(bs, rs)
        xa = x_ref[bs, rs, :] + (1.0 + g1_ref[bs]) * _rms(mx.reshape(pb, pr, d), npost_ref[...])
        x1.append(xa)
        hb.append(flat(_rms(xa, npre_ref[...]) * (1.0 + sc2_ref[bs]) + sh2_ref[bs]).astype(BF16))
    gt = [jnp.dot(h, wg_ref[...], preferred_element_type=F32) for h in hb]
    up = [jnp.dot(h, wu_ref[...], preferred_element_type=F32) for h in hb]
    for (bs, rs), g in zip(parts, gt):
        pb, pr = shape(bs, rs)
        gbuf[bs, PAD_ROWS + rs.start:PAD_ROWS + rs.stop, :] = g.reshape(pb, pr, f)
    fbnew_ref[...] = gbuf[:, lb:lb + PAD_ROWS, :]

    cw = cw_ref[...]
    out = []
    for (bs, rs), g, u in zip(parts, gt, up):
        pb, pr = shape(bs, rs)
        y = cb_ref[...] + g.reshape(pb, pr, f) * cw[FFN_CONV - 1:FFN_CONV]
        for j in range(FFN_CONV - 1):
            s = PAD_ROWS - (FFN_CONV - 1) + j
            y = y + gbuf[bs, s + rs.start:s + rs.stop, :] * cw[j:j + 1]
        act = flat(_silu(y)) * u
        out.append(jnp.dot(act.astype(BF16), wd_ref[...], preferred_element_type=F32))
    for (bs, rs), xa, o in zip(parts, x1, out):
        pb, pr = shape(bs, rs)
        xo_ref[bs, rs, :] = xa + (1.0 + g2_ref[bs]) * _rms(o.reshape(pb, pr, d), npostf_ref[...])


def _ffn_call(og, x, mod, npost, npre, npostf, wout, wg, wu, cw, cb, wd, fbuf, nb, lb):
    bsz, seq, d = x.shape
    f = wg.shape[1]
    row = lambda k: pl.BlockSpec((nb, 1, d), lambda b, t, k=k: (b, 0, k))
    vec = lambda n: pl.BlockSpec((1, n), lambda b, t: (0, 0))
    const = lambda r, c: pl.BlockSpec((r, c), lambda b, t: (0, 0), pipeline_mode=pl.Buffered(1))
    tok = pl.BlockSpec((nb, lb, d), lambda b, t: (b, t, 0))
    hist = pl.BlockSpec((nb, PAD_ROWS, f), lambda b, t: (b, 0, 0))
    return pl.pallas_call(
        _ffn_kernel,
        grid=(bsz // nb, seq // lb),
        in_specs=[tok, tok, row(2), row(3), row(4), row(5), vec(d), vec(d), vec(d),
                  const(d, d), const(d, f), const(d, f),
                  pl.BlockSpec((FFN_CONV, f), lambda b, t: (0, 0)), vec(f), const(f, d), hist],
        out_specs=[tok, hist],
        out_shape=[jax.ShapeDtypeStruct((bsz, seq, d), F32),
                   jax.ShapeDtypeStruct((bsz, PAD_ROWS, f), F32)],
        scratch_shapes=[pltpu.VMEM((nb, lb + PAD_ROWS, f), F32)],
        compiler_params=_params("arbitrary", "arbitrary"),
        name="ffn_block",
    )(og, x, mod, mod, mod, mod, npost.reshape(1, d), npre.reshape(1, d), npostf.reshape(1, d),
      wout, wg, wu, cw, cb.reshape(1, f), wd, fbuf)


def _seq_masks(nsq, lc):
    c = nsq * lc
    i = np.arange(c)[:, None]
    j = np.arange(c)[None, :]
    same = (i // lc) == (j // lc)
    return i, j, same


def _gdn_consts(nsq, lc):
    i, j, same = _seq_masks(nsq, lc)
    return jnp.asarray((same & (j <= i)).astype(np.float32), BF16)


def _hgrn_levels(lc):
    s, out = lc, []
    while s >= 4:
        out.append(s)
        s //= 2
    return out


def _hgrn_consts(nsq, lc):
    i, j, same = _seq_masks(nsq, lc)
    mats = [same & (j <= i), same & (j > i)]
    for s in _hgrn_levels(lc):
        mid = (i // s) * s + s // 2
        mats.append(np.where(i >= mid, (j >= mid) & (j <= i), (j > i) & (j < mid)))
    return jnp.asarray(np.concatenate(mats, axis=0).astype(np.float32), BF16)


def _state_index(ch, n, nsq, lc, lr):
    return ((ch * nsq + n) * lc) // lr


def _store_heads(og_ref, ch, nsq, lc, val):
    for h in range(HEADS):
        hs = slice(h * HEAD_DIM, (h + 1) * HEAD_DIM)
        if og_ref.shape[0] == 1:
            og_ref[0, ch * CHUNK:(ch + 1) * CHUNK, hs] = val[h]
        else:
            og_ref[ch * nsq:(ch + 1) * nsq, :, hs] = val[h].reshape(nsq, lc, HEAD_DIM)


def _gdn_kernel(qkv_ref, gt_ref, ba_ref, s0_ref, alog_ref, dtb_ref, ng_ref, tri_ref, *rest, nch, nsq, lc):
    og_ref, snew_ref, s_scr = rest[-3:]
    ns, lr, w3 = qkv_ref.shape
    r = ns * lr
    hd = HEAD_DIM
    qk = HEADS * hd
    t = pl.program_id(1)
    last = pl.num_programs(1) - 1

    @pl.when(t == 0)
    def _():
        s_scr[...] = s0_ref[...]

    qkv = qkv_ref[...].reshape(r, w3)
    gt = gt_ref[...].reshape(r, qk)

    ba = ba_ref[...].reshape(r, hd)
    beta = _sigmoid(ba)
    g = -jnp.exp(alog_ref[...]) * _softplus(ba + dtb_ref[...])
    chunks = [slice(ch * CHUNK, (ch + 1) * CHUNK) for ch in range(nch)]
    gcum = [_dot_sel(tri_ref[...], g[sl], 3) for sl in chunks]
    per_item = lambda f: jnp.concatenate([f(ch, h)[None] for ch in range(nch) for h in range(HEADS)], axis=0)
    gc = per_item(lambda ch, h: gcum[ch][:, HEADS + h:HEADS + h + 1])
    bcol = per_item(lambda ch, h: beta[chunks[ch], h:h + 1])

    q = _items(qkv, nch, 0, hd)
    k = _items(qkv, nch, qk, hd)
    v = _items(qkv, nch, 2 * qk, hd)
    eg = jnp.exp(gc)
    kb = k * bcol
    qg = q * eg
    rhs = jnp.concatenate([v * bcol, kb * eg], axis=-1).astype(BF16)
    gate = _silu(_items(gt, nch, 0, hd))
    rid = _idiv(lax.broadcasted_iota(jnp.int32, (1, CHUNK, 1), 1), lc)

    even = lambda x: jnp.concatenate([x[i:i + 1] for i in range(0, x.shape[0], 2)], axis=0)
    odd = lambda x: jnp.concatenate([x[i:i + 1] for i in range(1, x.shape[0], 2)], axis=0)
    side = lambda x: jnp.concatenate([even(x), odd(x)], axis=-1)
    stack = lambda x: jnp.concatenate(
        [jnp.concatenate([even(x), jnp.zeros_like(even(x))], axis=-1),
         jnp.concatenate([jnp.zeros_like(odd(x)), odd(x)], axis=-1)], axis=1)
    lane = lax.broadcasted_iota(jnp.int32, (1, 1, 2 * CHUNK), 2)
    left = lane < CHUNK
    col = lane & (CHUNK - 1)
    row = lax.broadcasted_iota(jnp.int32, (1, CHUNK, 1), 1)
    same = _idiv(row, lc) == _idiv(col, lc)
    incl = same & (col <= row)
    strict = same & (col < row)
    gcp = jnp.where(left, even(gc), odd(gc))
    grp = jnp.sum(jnp.where(col == row, gcp, 0.0), axis=1, keepdims=True)
    decay = jnp.where(incl, jnp.exp(jnp.where(incl, gcp - grp, 0.0)), 0.0)
    kst = stack(k.astype(BF16))
    a = jnp.where(strict, _bmm_nt(side(kb), kst) * decay, 0.0)
    attn = jnp.where(incl, _bmm_nt(side(q), kst) * decay, 0.0)
    tinv = _paired_unit_lower_inverse(a, min(16, lc), lc, left, row, col)
    th, tl = _split2(tinv)
    rst = stack(rhs)
    sol = _bmm(th, rst) + _bmm(tl, rst)
    heads_of = lambda x, off: jnp.concatenate(
        [x[p:p + 1, :, s * (x.shape[-1] // 2) + off:s * (x.shape[-1] // 2) + off + hd]
         for p in range(x.shape[0]) for s in range(2)], axis=0)
    uv, w = heads_of(sol, 0), heads_of(sol, hd)

    for ch in range(nch):
        it = slice(ch * HEADS, (ch + 1) * HEADS)
        pr = slice(ch * HEADS // 2, (ch + 1) * HEADS // 2)
        us, os_ = [], []
        for n in range(nsq):
            sl = slice(n * lc, (n + 1) * lc)
            st = s_scr[_state_index(ch, n, nsq, lc, lr)]
            ws = _bmm(jnp.concatenate([w[it, sl], qg[it, sl]], axis=1), st)
            us.append(uv[it, sl] - ws[:, :lc])
            os_.append(ws[:, lc:])
        u = us[0] if nsq == 1 else jnp.concatenate(us, axis=1)
        o = (os_[0] if nsq == 1 else jnp.concatenate(os_, axis=1)) + heads_of(_bmm(attn[pr], stack(u.astype(BF16))), 0)
        for n in range(nsq):
            si = _state_index(ch, n, nsq, lc, lr)
            gl = gc[it, (n + 1) * lc - 1:(n + 1) * lc, :]
            if nsq > 1:
                kd = jnp.where(rid == n, k[it] * jnp.exp(jnp.where(rid == n, gl - gc[it], 0.0)), 0.0)
            else:
                kd = k[it] * jnp.exp(gl - gc[it])
            s_scr[si] = s_scr[si] * jnp.exp(gl) + _bmm(jnp.swapaxes(kd, 1, 2), u)
        _store_heads(og_ref, ch, nsq, lc, _rms(o, ng_ref[...]) * gate[it])

    @pl.when(t == last)
    def _():
        for n in range(ns):
            _write_state(snew_ref, n, s_scr[n])


def _stacked_state(states, j, prev, ns):
    tail = states.shape[2:]
    in_spec = pl.BlockSpec((None, ns) + tail, lambda b, t: (j, b, 0, 0, 0))
    if prev is None:
        out_spec = pl.BlockSpec((states.shape[0], ns) + tail, lambda b, t: (0, b, 0, 0, 0))
        extra_in, extra_specs = [], []
    else:
        out_spec = in_spec
        extra_in, extra_specs = [prev], [pl.BlockSpec(memory_space=pl.ANY)]
    return in_spec, out_spec, extra_in, extra_specs, jax.ShapeDtypeStruct(states.shape, states.dtype)


def _write_state(snew_ref, n, value):
    if len(snew_ref.shape) == 5:
        for m in range(snew_ref.shape[0]):
            snew_ref[m, n] = value
    else:
        snew_ref[n] = value


def _gdn_call(qkv, gt, ba, states, j, prev, alog, dtb, ng, nch, nsq, lc):
    bsz, seq, w3 = qkv.shape
    qk = w3 // 3
    hd = HEAD_DIM
    tri = _gdn_consts(nsq, lc)
    ns, lr = (1, nch * CHUNK) if nsq == 1 else (nch * nsq, lc)
    tok = lambda n: pl.BlockSpec((ns, lr, n), lambda b, t: (b, t, 0))
    vec = lambda n: pl.BlockSpec((1, n), lambda b, t: (0, 0))
    state, state_out, extra_in, extra_specs, state_shape = _stacked_state(states, j, prev, ns)
    inputs = [qkv, gt, ba, states, alog, dtb, ng.reshape(1, hd), tri] + extra_in
    return pl.pallas_call(
        functools.partial(_gdn_kernel, nch=nch, nsq=nsq, lc=lc),
        grid=(bsz // ns, seq // lr),
        in_specs=[tok(w3), tok(qk), tok(hd), state, vec(hd), vec(hd), vec(hd),
                  pl.BlockSpec((CHUNK, CHUNK), lambda b, t: (0, 0))] + extra_specs,
        out_specs=[tok(qk), state_out],
        out_shape=[jax.ShapeDtypeStruct((bsz, seq, qk), F32), state_shape],
        input_output_aliases={len(inputs) - 1: 1} if extra_in else {},
        scratch_shapes=[pltpu.VMEM((ns, HEADS, hd, hd), F32)],
        compiler_params=_params("arbitrary", "arbitrary"),
        name="gdn_core",
    )(*inputs)


def _hgrn_kernel(proj_ref, s0_ref, lbraw_ref, ng_ref, cm_ref, *rest, nch, nsq, lc, layer):
    og_ref, snew_ref, st_scr = rest[-3:]
    ns, lr, w4 = proj_ref.shape
    r = ns * lr
    hd = HEAD_DIM
    qf = HEADS * hd
    t = pl.program_id(1)
    last = pl.num_programs(1) - 1

    @pl.when(t == 0)
    def _():
        for n in range(ns):
            st_scr[n] = jnp.swapaxes(s0_ref[n], 1, 2)

    lbraw = lbraw_ref[...]
    e = jnp.exp(lbraw - jnp.max(lbraw, axis=0, keepdims=True))
    p = e / jnp.sum(e, axis=0, keepdims=True)
    lb = jnp.zeros((1, qf), F32)
    for m in range(1, layer + 1):
        lb = lb + p[m:m + 1, :]

    proj = proj_ref[...].reshape(r, w4)
    fr = proj[:, qf:2 * qf]
    efr = jnp.exp(-jnp.abs(fr))
    big = 1.0 / (1.0 + efr)
    small = efr * big
    sig_pos = jnp.where(fr >= 0.0, big, small)
    sig_neg = jnp.where(fr >= 0.0, small, big)
    forget = jnp.maximum(lb, LB_FLOOR) + (1.0 - lb) * sig_pos
    logf = jnp.log(forget)

    levels = _hgrn_levels(lc)
    nmat = 2 + len(levels)
    cm = cm_ref[...]
    ex = jnp.concatenate([jnp.exp(_dot_sel(cm, logf[ch * CHUNK:(ch + 1) * CHUNK], 2)) for ch in range(nch)], axis=0)
    factor = lambda m: jnp.concatenate(
        [ex[(ch * nmat + m) * CHUNK:(ch * nmat + m + 1) * CHUNK, h * hd:(h + 1) * hd][None]
         for ch in range(nch) for h in range(HEADS)], axis=0)

    q = _silu(_items(proj, nch, 0, hd))
    k = _items((1.0 - lb) * sig_neg, nch, 0, hd)
    v = _items(proj, nch, 2 * qf, hd)
    gate = _silu(_items(proj, nch, 3 * qf, hd))

    row = lax.broadcasted_iota(jnp.int32, (CHUNK, CHUNK), 0)
    col = lax.broadcasted_iota(jnp.int32, (CHUNK, CHUNK), 1)
    qb, kb = q.astype(BF16), k.astype(BF16)
    attn = jnp.where((row == col)[None], _bmm_nt(qb, kb), 0.0)
    odd = (lax.broadcasted_iota(jnp.int32, (1, CHUNK, 1), 1) & 1) == 1
    factors = [jnp.where(odd, _items(forget, nch, 0, hd), 1.0)] + [factor(2 + li) for li in range(len(levels))]
    for s, fl in zip([2] + levels, factors):
        mask = (_idiv(row, s) == _idiv(col, s)) & ((row & (s - 1)) >= s // 2) & ((col & (s - 1)) < s // 2)
        flb = fl.astype(BF16)
        attn = attn + jnp.where(mask[None], _bmm_nt(qb * flb, kb * flb), 0.0)
    eg = factor(0)
    qe = q * eg
    kd = k * factor(1)
    cid = _idiv(lax.broadcasted_iota(jnp.int32, (1, 1, CHUNK), 2), lc)

    for ch in range(nch):
        it = slice(ch * HEADS, (ch + 1) * HEADS)
        vt = jnp.swapaxes(v[it], 1, 2)
        os_ = []
        for n in range(nsq):
            sl = slice(n * lc, (n + 1) * lc)
            si = _state_index(ch, n, nsq, lc, lr)
            st = st_scr[si]
            os_.append(_bmm_nt(qe[it, sl], st))
            vtn = vt if nsq == 1 else jnp.where(cid == n, vt, 0.0)
            egl = eg[it, (n + 1) * lc - 1:(n + 1) * lc, :]
            st_scr[si] = st * egl + _bmm(vtn, kd[it])
        o = (os_[0] if nsq == 1 else jnp.concatenate(os_, axis=1)) + _bmm(attn[it], v[it])
        _store_heads(og_ref, ch, nsq, lc, _rms(o, ng_ref[...]) * gate[it])

    @pl.when(t == last)
    def _():
        for n in range(ns):
            _write_state(snew_ref, n, jnp.swapaxes(st_scr[n], 1, 2))


def _hgrn_call(proj, states, j, prev, lbraw, ng, nch, nsq, lc):
    bsz, seq, w4 = proj.shape
    qf = w4 // 4
    hd = HEAD_DIM
    cm = _hgrn_consts(nsq, lc)
    ns, lr = (1, nch * CHUNK) if nsq == 1 else (nch * nsq, lc)
    tok = lambda n: pl.BlockSpec((ns, lr, n), lambda b, t: (b, t, 0))
    state, state_out, extra_in, extra_specs, state_shape = _stacked_state(states, j, prev, ns)
    inputs = [proj, states, lbraw, ng.reshape(1, hd), cm] + extra_in
    return pl.pallas_call(
        functools.partial(_hgrn_kernel, nch=nch, nsq=nsq, lc=lc, layer=j),
        grid=(bsz // ns, seq // lr),
        in_specs=[tok(w4), state,
                  pl.BlockSpec(lbraw.shape, lambda b, t: (0, 0)),
                  pl.BlockSpec((1, hd), lambda b, t: (0, 0)),
                  pl.BlockSpec(cm.shape, lambda b, t: (0, 0))] + extra_specs,
        out_specs=[tok(qf), state_out],
        out_shape=[jax.ShapeDtypeStruct((bsz, seq, qf), F32), state_shape],
        input_output_aliases={len(inputs) - 1: 1} if extra_in else {},
        scratch_shapes=[pltpu.VMEM((ns, HEADS, hd, hd), F32)],
        compiler_params=_params("arbitrary", "arbitrary"),
        name="hgrn_core",
    )(*inputs)


def _prep_weights(gdn_w_in, gdn_a_log, gdn_dt_bias, hgrn_w_in, gdn_w_out, hgrn_w_out, ffn_w_gu, ffn_w_down):
    pad = HEAD_DIM - 2 * HEADS
    w_gdn = jnp.pad(gdn_w_in, ((0, 0), (0, 0), (0, pad))).astype(BF16)
    gate_pad = lambda a: jnp.pad(a, ((0, 0), (HEADS, HEAD_DIM - 2 * HEADS)))[:, None, :]
    f = ffn_w_down.shape[1]
    return dict(
        w_gdn=w_gdn, alog=gate_pad(gdn_a_log), dtb=gate_pad(gdn_dt_bias),
        w_hgrn=hgrn_w_in.astype(BF16), gdn_w_out=gdn_w_out.astype(BF16), hgrn_w_out=hgrn_w_out.astype(BF16),
        wg=ffn_w_gu[:, :, :f].astype(BF16), wu=ffn_w_gu[:, :, f:].astype(BF16), wd=ffn_w_down.astype(BF16))


def _trunk(x, mod, s_gdn, s_gconv, s_hgrn, s_fconv, wts, p, nb, lb, nsq, lc):
    depth = mod.shape[0]
    bsz, seq, _ = x.shape
    qk = HEADS * HEAD_DIM
    if nsq == 1:
        chunks_avail = seq // CHUNK
    else:
        state_bytes = (3 + 2 * max(s_gdn.shape[0], s_hgrn.shape[0])) * nsq * HEADS * HEAD_DIM * HEAD_DIM * 4
        chunks_avail = max(1, min(bsz // nsq, STATE_VMEM_BUDGET // state_bytes))
    gdn_nch = min(GDN_CHUNKS, chunks_avail)
    hgrn_nch = min(HGRN_CHUNKS, chunks_avail)
    new_gdn, new_gconv, new_hgrn, new_fconv = None, [], None, []
    for layer in range(depth):
        modl = mod[layer][:, None, :]
        j = layer // N_MIXERS
        if layer % N_MIXERS == 0:
            cbuf = jnp.pad(s_gconv[j], ((0, 0), (PAD_ROWS - (GDN_CONV - 1), 0), (0, 0)))
            qkv, gt, ba, cb_new = _inproj_gdn_call(x, modl, wts["norm_pre_mix"][layer], p["w_gdn"][j],
                                                   wts["gdn_conv_w"][j], wts["gdn_conv_b"][j], cbuf, nb, lb)
            og, new_gdn = _gdn_call(qkv, gt, ba, s_gdn, j, new_gdn, p["alog"][j], p["dtb"][j],
                                    wts["gdn_norm"][j], gdn_nch, nsq, lc)
            new_gconv.append(cb_new[:, PAD_ROWS - (GDN_CONV - 1):, :])
            w_out = p["gdn_w_out"][j]
        else:
            (proj,) = _inproj_call(x, modl, wts["norm_pre_mix"][layer], p["w_hgrn"][j], (4 * qk,), nb, lb)
            og, new_hgrn = _hgrn_call(proj, s_hgrn, j, new_hgrn, wts["hgrn_lb"], wts["hgrn_norm"][j],
                                      hgrn_nch, nsq, lc)
            w_out = p["hgrn_w_out"][j]
        fbuf = jnp.pad(s_fconv[layer], ((0, 0), (PAD_ROWS - (FFN_CONV - 1), 0), (0, 0)))
        x, fb_new = _ffn_call(og, x, modl, wts["norm_post_mix"][layer], wts["norm_pre_ffn"][layer],
                              wts["norm_post_ffn"][layer], w_out, p["wg"][layer], p["wu"][layer],
                              wts["ffn_conv_w"][layer], wts["ffn_conv_b"][layer], p["wd"][layer], fbuf, nb, lb)
        new_fconv.append(fb_new[:, PAD_ROWS - (FFN_CONV - 1):, :])
    return x, new_gdn, jnp.stack(new_gconv), new_hgrn, jnp.stack(new_fconv)


def kernel(x_prompt, x_sample, state_gdn, state_gdn_conv, state_hgrn, state_ffn_conv, c_prompt, c_sample, ada_w, ada_b, norm_pre_mix, norm_post_mix, norm_pre_ffn, norm_post_ffn, gdn_w_in, gdn_conv_w, gdn_conv_b, gdn_a_log, gdn_dt_bias, gdn_norm, gdn_w_out, hgrn_lb, hgrn_w_in, hgrn_norm, hgrn_w_out, ffn_w_gu, ffn_conv_w, ffn_conv_b, ffn_w_down):
    wts = dict(norm_pre_mix=norm_pre_mix, norm_post_mix=norm_post_mix, norm_pre_ffn=norm_pre_ffn,
               norm_post_ffn=norm_post_ffn, gdn_conv_w=gdn_conv_w, gdn_conv_b=gdn_conv_b, gdn_norm=gdn_norm,
               hgrn_lb=hgrn_lb, hgrn_norm=hgrn_norm, ffn_conv_w=ffn_conv_w, ffn_conv_b=ffn_conv_b)
    p = _prep_weights(gdn_w_in, gdn_a_log, gdn_dt_bias, hgrn_w_in, gdn_w_out, hgrn_w_out, ffn_w_gu, ffn_w_down)

    bp, sp, _ = x_prompt.shape
    bs, ss, _ = x_sample.shape
    mod = _mod_call(jnp.concatenate([c_prompt, c_sample], axis=0), ada_w, ada_b)
    mod_p, mod_s = mod[:, :bp], mod[:, bp:]

    dt = x_prompt.dtype
    zeros = lambda a: jnp.zeros((a.shape[0], bp) + a.shape[2:], dt)
    y_p, p_gdn, p_gconv, p_hgrn, p_fconv = _trunk(
        x_prompt, mod_p, zeros(state_gdn), zeros(state_gdn_conv), zeros(state_hgrn), zeros(state_ffn_conv),
        wts, p, nb=1, lb=min(TILE_ROWS, sp), nsq=1, lc=min(CHUNK, sp))
    y_s, s_gdn, s_gconv, s_hgrn, s_fconv = _trunk(
        x_sample, mod_s, state_gdn, state_gdn_conv, state_hgrn, state_ffn_conv,
        wts, p, nb=min(bs, SHORT_TILE_ROWS // ss), lb=ss, nsq=min(bs, CHUNK // ss), lc=ss)
    return (y_p, y_s, p_gdn, p_gconv, p_hgrn, p_fconv, s_gdn, s_gconv, s_hgrn, s_fconv)
```

```python
import functools

import numpy as np
import jax
import jax.numpy as jnp
from jax import lax
from jax.experimental import pallas as pl
from jax.experimental.pallas import tpu as pltpu

F32 = jnp.float32
BF16 = jnp.bfloat16
EPS = 1e-6
LB_FLOOR = 1e-30
N_MOD = 6
N_MIXERS = 2
HEADS = 8
HEAD_DIM = 128
GDN_CONV = 4
FFN_CONV = 3
CHUNK = 64
PAD_ROWS = 8
TILE_ROWS = 512
SHORT_TILE_ROWS = 256
GDN_CHUNKS = 4
HGRN_CHUNKS = 4
ROW_PARTS = 2
VMEM_LIMIT = 56 * 1024 * 1024
STATE_VMEM_BUDGET = 32 * 1024 * 1024


def _params(*sem):
    return pltpu.CompilerParams(dimension_semantics=sem, vmem_limit_bytes=VMEM_LIMIT)


def _sigmoid(x):
    return jax.nn.sigmoid(x)


def _silu(x):
    hx = 0.5 * x
    return hx + hx * jnp.tanh(hx)


def _softplus(x):
    return jnp.maximum(x, 0.0) + jnp.log1p(jnp.exp(-jnp.abs(x)))


def _rms(x, g):
    return x * lax.rsqrt(jnp.mean(x * x, axis=-1, keepdims=True) + EPS) * g


def _dot(a, b):
    return jnp.dot(a.astype(BF16), b.astype(BF16), preferred_element_type=F32)


def _bmm(a, b):
    return lax.dot_general(a.astype(BF16), b.astype(BF16), (((2,), (1,)), ((0,), (0,))),
                           preferred_element_type=F32)


def _bmm_nt(a, b):
    return lax.dot_general(a.astype(BF16), b.astype(BF16), (((2,), (2,)), ((0,), (0,))),
                           preferred_element_type=F32)


def _split2(x):
    hi = x.astype(BF16)
    lo = (x - hi.astype(F32)).astype(BF16)
    return hi, lo


def _pair_product(lhs, rhs, left):
    c = rhs.shape[1]
    n = len(lhs)
    rh, rl = _split2(rhs)
    bd = lambda x: jnp.concatenate([jnp.where(left, x, jnp.zeros_like(x)),
                                    jnp.where(left, jnp.zeros_like(x), x)], axis=1)
    pieces = [_split2(x) for x in lhs]
    his = [hi for hi, _ in pieces]
    los = [lo for _, lo in pieces]
    top = _bmm(jnp.concatenate(his + los, axis=1), bd(rh))
    bot = _bmm(his[0] if n == 1 else jnp.concatenate(his, axis=1), bd(rl))
    return [top[:, i * c:(i + 1) * c] + top[:, (n + i) * c:(n + i + 1) * c] + bot[:, i * c:(i + 1) * c]
            for i in range(n)]


def _dot_sel(sel, x, pieces):
    acc = None
    r = x
    for i in range(pieces):
        part = r.astype(BF16)
        if i + 1 < pieces:
            r = r - part.astype(F32)
        term = jnp.dot(sel, part, preferred_element_type=F32)
        acc = term if acc is None else acc + term
    return acc


def _idiv(x, p2):
    return x >> (p2.bit_length() - 1)


def _paired_unit_lower_inverse(a, block, span, left, row, col):
    eye = (col == row).astype(F32)
    if block >= span:
        return _paired_neumann(eye, a, span, left)
    diag = _idiv(row, block) == _idiv(col, block)
    inv = _paired_neumann(eye, jnp.where(diag, a, 0.0), block, left)
    (low,) = _pair_product([inv], jnp.where(diag, 0.0, a), left)
    blk = _paired_neumann(eye, low, span // block, left)
    (res,) = _pair_product([blk], inv, left)
    return res


def _paired_neumann(eye, d, order, left):
    p = eye - d
    if order <= 2:
        return p
    (dp,) = _pair_product([d], d, left)
    k = 4
    while k < order:
        pp, dp_next = _pair_product([p, dp], dp, left)
        p, dp = p + pp, dp_next
        k *= 2
    (pp,) = _pair_product([p], dp, left)
    return p + pp


def _items(x, nch, base, width):
    return jnp.concatenate(
        [x[ch * CHUNK:(ch + 1) * CHUNK, base + h * width:base + (h + 1) * width][None]
         for ch in range(nch) for h in range(HEADS)], axis=0)


def _mod_kernel(c_ref, w_ref, b_ref, o_ref):
    cs = _silu(c_ref[...])
    o_ref[0] = _dot(cs, w_ref[0]) + b_ref[0]


def _mod_call(c_all, ada_w, ada_b):
    depth, d, n = ada_w.shape
    rows = c_all.shape[0]
    tn = n // 4
    return pl.pallas_call(
        _mod_kernel,
        grid=(depth, n // tn),
        in_specs=[pl.BlockSpec((rows, d), lambda l, j: (0, 0)),
                  pl.BlockSpec((1, d, tn), lambda l, j: (l, 0, j)),
                  pl.BlockSpec((1, 1, tn), lambda l, j: (l, 0, j))],
        out_specs=pl.BlockSpec((1, rows, tn), lambda l, j: (l, 0, j)),
        out_shape=jax.ShapeDtypeStruct((depth, rows, n), F32),
        compiler_params=_params("arbitrary", "arbitrary"),
        name="adaln_mod",
    )(c_all, ada_w, ada_b.reshape(depth, 1, n))


def _row_parts(nb, lb, count):
    if nb >= count:
        return [(slice(i * nb // count, (i + 1) * nb // count), slice(0, lb)) for i in range(count)]
    return [(slice(0, nb), slice(i * lb // count, (i + 1) * lb // count)) for i in range(count)]


def _flat(v):
    return v.reshape(v.shape[0] * v.shape[1], v.shape[2])


def _inproj_kernel(x_ref, sh_ref, sc_ref, g_ref, w_ref, *out_refs, splits):
    nb, lb, d = x_ref.shape
    parts = _row_parts(nb, lb, ROW_PARTS)
    hbs = [_flat(_rms(x_ref[bs, rs, :], g_ref[...]) * (1.0 + sc_ref[bs]) + sh_ref[bs]).astype(BF16)
           for bs, rs in parts]
    off = 0
    for o_ref, n in zip(out_refs, splits):
        for (bs, rs), hb in zip(parts, hbs):
            res = jnp.dot(hb, w_ref[:, off:off + n], preferred_element_type=F32)
            o_ref[bs, rs, :] = res.reshape(bs.stop - bs.start, rs.stop - rs.start, n)
        off += n


def _inproj_gdn_kernel(x_ref, sh_ref, sc_ref, g_ref, w_ref, cw_ref, cb_ref, cbuf_ref,
                       qkv_ref, gt_ref, ba_ref, cbnew_ref, xbuf):
    nb, lb, d = x_ref.shape
    hd = HEAD_DIM
    qk = HEADS * hd
    w3 = 3 * qk
    t = pl.program_id(1)
    parts = _row_parts(nb, lb, ROW_PARTS)
    shape = lambda bs, rs: (bs.stop - bs.start, rs.stop - rs.start)

    @pl.when(t == 0)
    def _():
        xbuf[:, PAD_ROWS - (GDN_CONV - 1):PAD_ROWS, :] = cbuf_ref[...]

    @pl.when(t > 0)
    def _():
        xbuf[:, 0:PAD_ROWS, :] = xbuf[:, lb:lb + PAD_ROWS, :]

    hbs = [_flat(_rms(x_ref[bs, rs, :], g_ref[...]) * (1.0 + sc_ref[bs]) + sh_ref[bs]).astype(BF16)
           for bs, rs in parts]
    raws = [jnp.dot(hb, w_ref[:, 0:w3], preferred_element_type=F32) for hb in hbs]
    for (bs, rs), raw in zip(parts, raws):
        xbuf[bs, PAD_ROWS + rs.start:PAD_ROWS + rs.stop, :] = raw.reshape(shape(bs, rs) + (w3,))
    cbnew_ref[...] = xbuf[:, lb + PAD_ROWS - (GDN_CONV - 1):lb + PAD_ROWS, :]

    cw = cw_ref[...]
    for (bs, rs), hb, raw in zip(parts, hbs, raws):
        pb, pr = shape(bs, rs)
        y = cb_ref[...] + raw.reshape(pb, pr, w3) * cw[GDN_CONV - 1:GDN_CONV]
        for j in range(GDN_CONV - 1):
            s = PAD_ROWS - (GDN_CONV - 1) + j
            y = y + xbuf[bs, s + rs.start:s + rs.stop, :] * cw[j:j + 1]
        a = _silu(y)
        for h in range(2 * HEADS):
            hs = slice(h * hd, (h + 1) * hd)
            v = a[:, :, hs]
            scale = lax.rsqrt(jnp.sum(v * v, axis=-1, keepdims=True) + EPS)
            qkv_ref[bs, rs, hs] = v * (scale * (hd ** -0.5) if h < HEADS else scale)
        qkv_ref[bs, rs, 2 * qk:w3] = a[:, :, 2 * qk:w3]
        gt_ref[bs, rs, :] = jnp.dot(hb, w_ref[:, w3:w3 + qk], preferred_element_type=F32).reshape(pb, pr, qk)
        ba_ref[bs, rs, :] = jnp.dot(hb, w_ref[:, w3 + qk:], preferred_element_type=F32).reshape(pb, pr, hd)


def _inproj_gdn_call(x, mod, gain, w, j, cw, cb, cbuf, nb, lb):
    bsz, seq, d = x.shape
    ntot = w.shape[2]
    qk = HEADS * HEAD_DIM
    w3 = 3 * qk
    row = lambda k: pl.BlockSpec((nb, 1, d), lambda b, t, k=k: (b, 0, k))
    tok = lambda n: pl.BlockSpec((nb, lb, n), lambda b, t: (b, t, 0))
    taps = GDN_CONV - 1
    return pl.pallas_call(
        _inproj_gdn_kernel,
        grid=(bsz // nb, seq // lb),
        in_specs=[tok(d), row(0), row(1),
                  pl.BlockSpec((1, d), lambda b, t: (0, 0)),
                  _layer_weight(d, ntot, j),
                  pl.BlockSpec((GDN_CONV, w3), lambda b, t: (0, 0)),
                  pl.BlockSpec((1, w3), lambda b, t: (0, 0)),
                  pl.BlockSpec((None, nb, taps, w3), lambda b, t: (j, b, 0, 0))],
        out_specs=[tok(w3), tok(qk), tok(HEAD_DIM), pl.BlockSpec((nb, taps, w3), lambda b, t: (b, 0, 0))],
        out_shape=[jax.ShapeDtypeStruct((bsz, seq, w3), F32), jax.ShapeDtypeStruct((bsz, seq, qk), F32),
                   jax.ShapeDtypeStruct((bsz, seq, HEAD_DIM), F32),
                   jax.ShapeDtypeStruct((bsz, taps, w3), F32)],
        scratch_shapes=[pltpu.VMEM((nb, lb + PAD_ROWS, w3), F32)],
        compiler_params=_params("arbitrary", "arbitrary"),
        name="inproj_gdn",
    )(x, mod, mod, gain.reshape(1, d), w, cw, cb.reshape(1, w3), cbuf)


def _inproj_call(x, mod, gain, w, j, splits, nb, lb):
    bsz, seq, d = x.shape
    ntot = w.shape[2]
    row = lambda k: pl.BlockSpec((nb, 1, d), lambda b, t, k=k: (b, 0, k))
    return pl.pallas_call(
        functools.partial(_inproj_kernel, splits=splits),
        grid=(bsz // nb, seq // lb),
        in_specs=[pl.BlockSpec((nb, lb, d), lambda b, t: (b, t, 0)),
                  row(0), row(1),
                  pl.BlockSpec((1, d), lambda b, t: (0, 0)),
                  _layer_weight(d, ntot, j)],
        out_specs=[pl.BlockSpec((nb, lb, n), lambda b, t: (b, t, 0)) for n in splits],
        out_shape=[jax.ShapeDtypeStruct((bsz, seq, n), F32) for n in splits],
        compiler_params=_params("arbitrary", "arbitrary"),
        name="inproj",
    )(x, mod, mod, gain.reshape(1, d), w)


def _ffn_kernel(og_ref, x_ref, g1_ref, sh2_ref, sc2_ref, g2_ref, npost_ref, npre_ref, npostf_ref,
                wout_ref, wg_ref, wu_ref, cw_ref, cb_ref, wd_ref, fbuf_ref,
                xo_ref, fbnew_ref, gbuf):
    nb, lb, d = x_ref.shape
    f = wg_ref.shape[1]
    t = pl.program_id(1)
    parts = _row_parts(nb, lb, ROW_PARTS)
    shape = lambda bs, rs: (bs.stop - bs.start, rs.stop - rs.start)
    flat = _flat

    @pl.when(t == 0)
    def _():
        gbuf[:, PAD_ROWS - (FFN_CONV - 1):PAD_ROWS, :] = fbuf_ref[...]

    @pl.when(t > 0)
    def _():
        gbuf[:, 0:PAD_ROWS, :] = gbuf[:, lb:lb + PAD_ROWS, :]

    mix = [jnp.dot(flat(og_ref[bs, rs, :]).astype(BF16), wout_ref[...], preferred_element_type=F32)
           for bs, rs in parts]
    x1, hb = [], []
    for (bs, rs), mx in zip(parts, mix):
        pb, pr = shape(bs, rs)
        xa = x_ref[bs, rs, :] + (1.0 + g1_ref[bs]) * _rms(mx.reshape(pb, pr, d), npost_ref[...])
        x1.append(xa)
        hb.append(flat(_rms(xa, npre_ref[...]) * (1.0 + sc2_ref[bs]) + sh2_ref[bs]).astype(BF16))
    gt = [jnp.dot(h, wg_ref[...], preferred_element_type=F32) for h in hb]
    up = [jnp.dot(h, wu_ref[...], preferred_element_type=F32) for h in hb]
    for (bs, rs), g in zip(parts, gt):
        pb, pr = shape(bs, rs)
        gbuf[bs, PAD_ROWS + rs.start:PAD_ROWS + rs.stop, :] = g.reshape(pb, pr, f)
    fbnew_ref[...] = gbuf[:, lb + PAD_ROWS - (FFN_CONV - 1):lb + PAD_ROWS, :]

    cw = cw_ref[...]
    out = []
    for (bs, rs), g, u in zip(parts, gt, up):
        pb, pr = shape(bs, rs)
        y = cb_ref[...] + g.reshape(pb, pr, f) * cw[FFN_CONV - 1:FFN_CONV]
        for j in range(FFN_CONV - 1):
            s = PAD_ROWS - (FFN_CONV - 1) + j
            y = y + gbuf[bs, s + rs.start:s + rs.stop, :] * cw[j:j + 1]
        act = flat(_silu(y)) * u
        out.append(jnp.dot(act.astype(BF16), wd_ref[...], preferred_element_type=F32))
    for (bs, rs), xa, o in zip(parts, x1, out):
        pb, pr = shape(bs, rs)
        xo_ref[bs, rs, :] = xa + (1.0 + g2_ref[bs]) * _rms(o.reshape(pb, pr, d), npostf_ref[...])


def _layer_weight(rows, cols, layer, col_block=0):
    return pl.BlockSpec((None, rows, cols), lambda b, t: (layer, 0, col_block), pipeline_mode=pl.Buffered(1))


def _ffn_call(og, x, mod, npost, npre, npostf, wout, jw, wgu, cw, cb, wd, layer, fbuf, nb, lb):
    bsz, seq, d = x.shape
    f = wd.shape[1]
    row = lambda k: pl.BlockSpec((nb, 1, d), lambda b, t, k=k: (b, 0, k))
    vec = lambda n: pl.BlockSpec((1, n), lambda b, t: (0, 0))
    tok = pl.BlockSpec((nb, lb, d), lambda b, t: (b, t, 0))
    taps = FFN_CONV - 1
    return pl.pallas_call(
        _ffn_kernel,
        grid=(bsz // nb, seq // lb),
        in_specs=[tok, tok, row(2), row(3), row(4), row(5), vec(d), vec(d), vec(d),
                  _layer_weight(d, d, jw), _layer_weight(d, f, layer, 0), _layer_weight(d, f, layer, 1),
                  pl.BlockSpec((FFN_CONV, f), lambda b, t: (0, 0)), vec(f), _layer_weight(f, d, layer),
                  pl.BlockSpec((None, nb, taps, f), lambda b, t: (layer, b, 0, 0))],
        out_specs=[tok, pl.BlockSpec((nb, taps, f), lambda b, t: (b, 0, 0))],
        out_shape=[jax.ShapeDtypeStruct((bsz, seq, d), F32),
                   jax.ShapeDtypeStruct((bsz, taps, f), F32)],
        scratch_shapes=[pltpu.VMEM((nb, lb + PAD_ROWS, f), F32)],
        compiler_params=_params("arbitrary", "arbitrary"),
        name="ffn_block",
    )(og, x, mod, mod, mod, mod, npost.reshape(1, d), npre.reshape(1, d), npostf.reshape(1, d),
      wout, wgu, wgu, cw, cb.reshape(1, f), wd, fbuf)


def _seq_masks(nsq, lc):
    c = nsq * lc
    i = np.arange(c)[:, None]
    j = np.arange(c)[None, :]
    same = (i // lc) == (j // lc)
    return i, j, same


def _gdn_consts(nsq, lc):
    i, j, same = _seq_masks(nsq, lc)
    return jnp.asarray((same & (j <= i)).astype(np.float32), BF16)


def _hgrn_levels(lc):
    s, out = lc, []
    while s >= 4:
        out.append(s)
        s //= 2
    return out


def _hgrn_consts(nsq, lc):
    i, j, same = _seq_masks(nsq, lc)
    mats = [same & (j <= i), same & (j > i)]
    for s in _hgrn_levels(lc):
        mid = (i // s) * s + s // 2
        mats.append(np.where(i >= mid, (j >= mid) & (j <= i), (j > i) & (j < mid)))
    return jnp.asarray(np.concatenate(mats, axis=0).astype(np.float32), BF16)


def _state_index(ch, n, nsq, lc, lr):
    return ((ch * nsq + n) * lc) // lr


def _store_heads(og_ref, ch, nsq, lc, val):
    for h in range(HEADS):
        hs = slice(h * HEAD_DIM, (h + 1) * HEAD_DIM)
        if og_ref.shape[0] == 1:
            og_ref[0, ch * CHUNK:(ch + 1) * CHUNK, hs] = val[h]
        else:
            og_ref[ch * nsq:(ch + 1) * nsq, :, hs] = val[h].reshape(nsq, lc, HEAD_DIM)


def _gdn_kernel(qkv_ref, gt_ref, ba_ref, s0_ref, alog_ref, dtb_ref, ng_ref, tri_ref, *rest, nch, nsq, lc):
    og_ref, snew_ref, s_scr = rest[-3:]
    ns, lr, w3 = qkv_ref.shape
    r = ns * lr
    hd = HEAD_DIM
    qk = HEADS * hd
    t = pl.program_id(1)
    last = pl.num_programs(1) - 1

    @pl.when(t == 0)
    def _():
        s_scr[...] = s0_ref[...]

    qkv = qkv_ref[...].reshape(r, w3)
    gt = gt_ref[...].reshape(r, qk)

    ba = ba_ref[...].reshape(r, hd)
    beta = _sigmoid(ba)
    g = -jnp.exp(alog_ref[...]) * _softplus(ba + dtb_ref[...])
    chunks = [slice(ch * CHUNK, (ch + 1) * CHUNK) for ch in range(nch)]
    gcum = [_dot_sel(tri_ref[...], g[sl], 3) for sl in chunks]
    per_item = lambda f: jnp.concatenate([f(ch, h)[None] for ch in range(nch) for h in range(HEADS)], axis=0)
    gc = per_item(lambda ch, h: gcum[ch][:, HEADS + h:HEADS + h + 1])
    bcol = per_item(lambda ch, h: beta[chunks[ch], h:h + 1])

    q = _items(qkv, nch, 0, hd)
    k = _items(qkv, nch, qk, hd)
    v = _items(qkv, nch, 2 * qk, hd)
    eg = jnp.exp(gc)
    kb = k * bcol
    qg = q * eg
    rhs = jnp.concatenate([v * bcol, kb * eg], axis=-1).astype(BF16)
    gate = _silu(_items(gt, nch, 0, hd))
    rid = _idiv(lax.broadcasted_iota(jnp.int32, (1, CHUNK, 1), 1), lc)

    even = lambda x: jnp.concatenate([x[i:i + 1] for i in range(0, x.shape[0], 2)], axis=0)
    odd = lambda x: jnp.concatenate([x[i:i + 1] for i in range(1, x.shape[0], 2)], axis=0)
    side = lambda x: jnp.concatenate([even(x), odd(x)], axis=-1)
    stack = lambda x: jnp.concatenate(
        [jnp.concatenate([even(x), jnp.zeros_like(even(x))], axis=-1),
         jnp.concatenate([jnp.zeros_like(odd(x)), odd(x)], axis=-1)], axis=1)
    lane = lax.broadcasted_iota(jnp.int32, (1, 1, 2 * CHUNK), 2)
    left = lane < CHUNK
    col = lane & (CHUNK - 1)
    row = lax.broadcasted_iota(jnp.int32, (1, CHUNK, 1), 1)
    same = _idiv(row, lc) == _idiv(col, lc)
    incl = same & (col <= row)
    strict = same & (col < row)
    gcp = jnp.where(left, even(gc), odd(gc))
    grp = jnp.sum(jnp.where(col == row, gcp, 0.0), axis=1, keepdims=True)
    decay = jnp.where(incl, jnp.exp(jnp.where(incl, gcp - grp, 0.0)), 0.0)
    kst = stack(k.astype(BF16))
    a = jnp.where(strict, _bmm_nt(side(kb), kst) * decay, 0.0)
    attn = jnp.where(incl, _bmm_nt(side(q), kst) * decay, 0.0)
    tinv = _paired_unit_lower_inverse(a, min(16, lc), lc, left, row, col)
    th, tl = _split2(tinv)
    rst = stack(rhs)
    sol = _bmm(th, rst) + _bmm(tl, rst)
    heads_of = lambda x, off: jnp.concatenate(
        [x[p:p + 1, :, s * (x.shape[-1] // 2) + off:s * (x.shape[-1] // 2) + off + hd]
         for p in range(x.shape[0]) for s in range(2)], axis=0)
    uv, w = heads_of(sol, 0), heads_of(sol, hd)

    for ch in range(nch):
        it = slice(ch * HEADS, (ch + 1) * HEADS)
        pr = slice(ch * HEADS // 2, (ch + 1) * HEADS // 2)
        us, os_ = [], []
        for n in range(nsq):
            sl = slice(n * lc, (n + 1) * lc)
            st = s_scr[_state_index(ch, n, nsq, lc, lr)]
            ws = _bmm(jnp.concatenate([w[it, sl], qg[it, sl]], axis=1), st)
            us.append(uv[it, sl] - ws[:, :lc])
            os_.append(ws[:, lc:])
        u = us[0] if nsq == 1 else jnp.concatenate(us, axis=1)
        o = (os_[0] if nsq == 1 else jnp.concatenate(os_, axis=1)) + heads_of(_bmm(attn[pr], stack(u.astype(BF16))), 0)
        for n in range(nsq):
            si = _state_index(ch, n, nsq, lc, lr)
            gl = gc[it, (n + 1) * lc - 1:(n + 1) * lc, :]
            if nsq > 1:
                kd = jnp.where(rid == n, k[it] * jnp.exp(jnp.where(rid == n, gl - gc[it], 0.0)), 0.0)
            else:
                kd = k[it] * jnp.exp(gl - gc[it])
            s_scr[si] = s_scr[si] * jnp.exp(gl) + _bmm(jnp.swapaxes(kd, 1, 2), u)
        _store_heads(og_ref, ch, nsq, lc, _rms(o, ng_ref[...]) * gate[it])

    @pl.when(t == last)
    def _():
        for n in range(ns):
            _write_state(snew_ref, n, s_scr[n])


def _stacked_state(states, j, prev, ns):
    tail = states.shape[2:]
    in_spec = pl.BlockSpec((None, ns) + tail, lambda b, t: (j, b, 0, 0, 0))
    if prev is None:
        out_spec = pl.BlockSpec((states.shape[0], ns) + tail, lambda b, t: (0, b, 0, 0, 0))
        extra_in, extra_specs = [], []
    else:
        out_spec = in_spec
        extra_in, extra_specs = [prev], [pl.BlockSpec(memory_space=pl.ANY)]
    return in_spec, out_spec, extra_in, extra_specs, jax.ShapeDtypeStruct(states.shape, states.dtype)


def _write_state(snew_ref, n, value):
    if len(snew_ref.shape) == 5:
        for m in range(snew_ref.shape[0]):
            snew_ref[m, n] = value
    else:
        snew_ref[n] = value


def _gdn_call(qkv, gt, ba, states, j, prev, alog, dtb, ng, nch, nsq, lc):
    bsz, seq, w3 = qkv.shape
    qk = w3 // 3
    hd = HEAD_DIM
    tri = _gdn_consts(nsq, lc)
    ns, lr = (1, nch * CHUNK) if nsq == 1 else (nch * nsq, lc)
    tok = lambda n: pl.BlockSpec((ns, lr, n), lambda b, t: (b, t, 0))
    vec = lambda n: pl.BlockSpec((1, n), lambda b, t: (0, 0))
    state, state_out, extra_in, extra_specs, state_shape = _stacked_state(states, j, prev, ns)
    inputs = [qkv, gt, ba, states, alog, dtb, ng.reshape(1, hd), tri] + extra_in
    return pl.pallas_call(
        functools.partial(_gdn_kernel, nch=nch, nsq=nsq, lc=lc),
        grid=(bsz // ns, seq // lr),
        in_specs=[tok(w3), tok(qk), tok(hd), state, vec(hd), vec(hd), vec(hd),
                  pl.BlockSpec((CHUNK, CHUNK), lambda b, t: (0, 0))] + extra_specs,
        out_specs=[tok(qk), state_out],
        out_shape=[jax.ShapeDtypeStruct((bsz, seq, qk), F32), state_shape],
        input_output_aliases={len(inputs) - 1: 1} if extra_in else {},
        scratch_shapes=[pltpu.VMEM((ns, HEADS, hd, hd), F32)],
        compiler_params=_params("arbitrary", "arbitrary"),
        name="gdn_core",
    )(*inputs)


def _hgrn_kernel(proj_ref, s0_ref, lbraw_ref, ng_ref, cm_ref, *rest, nch, nsq, lc, layer):
    og_ref, snew_ref, st_scr = rest[-3:]
    ns, lr, w4 = proj_ref.shape
    r = ns * lr
    hd = HEAD_DIM
    qf = HEADS * hd
    t = pl.program_id(1)
    last = pl.num_programs(1) - 1

    @pl.when(t == 0)
    def _():
        for n in range(ns):
            st_scr[n] = jnp.swapaxes(s0_ref[n], 1, 2)

    lbraw = lbraw_ref[...]
    e = jnp.exp(lbraw - jnp.max(lbraw, axis=0, keepdims=True))
    p = e / jnp.sum(e, axis=0, keepdims=True)
    lb = jnp.zeros((1, qf), F32)
    for m in range(1, layer + 1):
        lb = lb + p[m:m + 1, :]

    proj = proj_ref[...].reshape(r, w4)
    fr = proj[:, qf:2 * qf]
    efr = jnp.exp(-jnp.abs(fr))
    big = 1.0 / (1.0 + efr)
    small = efr * big
    sig_pos = jnp.where(fr >= 0.0, big, small)
    sig_neg = jnp.where(fr >= 0.0, small, big)
    forget = jnp.maximum(lb, LB_FLOOR) + (1.0 - lb) * sig_pos
    logf = jnp.log(forget)

    levels = _hgrn_levels(lc)
    nmat = 2 + len(levels)
    cm = cm_ref[...]
    ex = jnp.concatenate([jnp.exp(_dot_sel(cm, logf[ch * CHUNK:(ch + 1) * CHUNK], 2)) for ch in range(nch)], axis=0)
    factor = lambda m: jnp.concatenate(
        [ex[(ch * nmat + m) * CHUNK:(ch * nmat + m + 1) * CHUNK, h * hd:(h + 1) * hd][None]
         for ch in range(nch) for h in range(HEADS)], axis=0)

    q = _silu(_items(proj, nch, 0, hd))
    k = _items((1.0 - lb) * sig_neg, nch, 0, hd)
    v = _items(proj, nch, 2 * qf, hd)
    gate = _silu(_items(proj, nch, 3 * qf, hd))

    row = lax.broadcasted_iota(jnp.int32, (CHUNK, CHUNK), 0)
    col = lax.broadcasted_iota(jnp.int32, (CHUNK, CHUNK), 1)
    qb, kb = q.astype(BF16), k.astype(BF16)
    attn = jnp.where((row == col)[None], _bmm_nt(qb, kb), 0.0)
    odd = (lax.broadcasted_iota(jnp.int32, (1, CHUNK, 1), 1) & 1) == 1
    factors = [jnp.where(odd, _items(forget, nch, 0, hd), 1.0)] + [factor(2 + li) for li in range(len(levels))]
    for s, fl in zip([2] + levels, factors):
        mask = (_idiv(row, s) == _idiv(col, s)) & ((row & (s - 1)) >= s // 2) & ((col & (s - 1)) < s // 2)
        flb = fl.astype(BF16)
        attn = attn + jnp.where(mask[None], _bmm_nt(qb * flb, kb * flb), 0.0)
    eg = factor(0)
    qe = q * eg
    kd = k * factor(1)
    cid = _idiv(lax.broadcasted_iota(jnp.int32, (1, 1, CHUNK), 2), lc)

    for ch in range(nch):
        it = slice(ch * HEADS, (ch + 1) * HEADS)
        vt = jnp.swapaxes(v[it], 1, 2)
        os_ = []
        for n in range(nsq):
            sl = slice(n * lc, (n + 1) * lc)
            si = _state_index(ch, n, nsq, lc, lr)
            st = st_scr[si]
            os_.append(_bmm_nt(qe[it, sl], st))
            vtn = vt if nsq == 1 else jnp.where(cid == n, vt, 0.0)
            egl = eg[it, (n + 1) * lc - 1:(n + 1) * lc, :]
            st_scr[si] = st * egl + _bmm(vtn, kd[it])
        o = (os_[0] if nsq == 1 else jnp.concatenate(os_, axis=1)) + _bmm(attn[it], v[it])
        _store_heads(og_ref, ch, nsq, lc, _rms(o, ng_ref[...]) * gate[it])

    @pl.when(t == last)
    def _():
        for n in range(ns):
            _write_state(snew_ref, n, jnp.swapaxes(st_scr[n], 1, 2))


def _hgrn_call(proj, states, j, prev, lbraw, ng, nch, nsq, lc):
    bsz, seq, w4 = proj.shape
    qf = w4 // 4
    hd = HEAD_DIM
    cm = _hgrn_consts(nsq, lc)
    ns, lr = (1, nch * CHUNK) if nsq == 1 else (nch * nsq, lc)
    tok = lambda n: pl.BlockSpec((ns, lr, n), lambda b, t: (b, t, 0))
    state, state_out, extra_in, extra_specs, state_shape = _stacked_state(states, j, prev, ns)
    inputs = [proj, states, lbraw, ng.reshape(1, hd), cm] + extra_in
    return pl.pallas_call(
        functools.partial(_hgrn_kernel, nch=nch, nsq=nsq, lc=lc, layer=j),
        grid=(bsz // ns, seq // lr),
        in_specs=[tok(w4), state,
                  pl.BlockSpec(lbraw.shape, lambda b, t: (0, 0)),
                  pl.BlockSpec((1, hd), lambda b, t: (0, 0)),
                  pl.BlockSpec(cm.shape, lambda b, t: (0, 0))] + extra_specs,
        out_specs=[tok(qf), state_out],
        out_shape=[jax.ShapeDtypeStruct((bsz, seq, qf), F32), state_shape],
        input_output_aliases={len(inputs) - 1: 1} if extra_in else {},
        scratch_shapes=[pltpu.VMEM((ns, HEADS, hd, hd), F32)],
        compiler_params=_params("arbitrary", "arbitrary"),
        name="hgrn_core",
    )(*inputs)


def _prep_weights(gdn_w_in, gdn_a_log, gdn_dt_bias, hgrn_w_in, gdn_w_out, hgrn_w_out, ffn_w_gu, ffn_w_down):
    pad = HEAD_DIM - 2 * HEADS
    w_gdn = jnp.pad(gdn_w_in, ((0, 0), (0, 0), (0, pad))).astype(BF16)
    gate_pad = lambda a: jnp.pad(a, ((0, 0), (HEADS, HEAD_DIM - 2 * HEADS)))[:, None, :]
    return dict(
        w_gdn=w_gdn, alog=gate_pad(gdn_a_log), dtb=gate_pad(gdn_dt_bias),
        w_hgrn=hgrn_w_in.astype(BF16), gdn_w_out=gdn_w_out.astype(BF16), hgrn_w_out=hgrn_w_out.astype(BF16),
        wgu=ffn_w_gu.astype(BF16), wd=ffn_w_down.astype(BF16))


def _trunk(x, mod, s_gdn, s_gconv, s_hgrn, s_fconv, wts, p, nb, lb, nsq, lc):
    depth = mod.shape[0]
    bsz, seq, _ = x.shape
    qk = HEADS * HEAD_DIM
    if nsq == 1:
        chunks_avail = seq // CHUNK
    else:
        state_bytes = (3 + 2 * max(s_gdn.shape[0], s_hgrn.shape[0])) * nsq * HEADS * HEAD_DIM * HEAD_DIM * 4
        chunks_avail = max(1, min(bsz // nsq, STATE_VMEM_BUDGET // state_bytes))
    gdn_nch = min(GDN_CHUNKS, chunks_avail)
    hgrn_nch = min(HGRN_CHUNKS, chunks_avail)
    new_gdn, new_gconv, new_hgrn, new_fconv = None, [], None, []
    for layer in range(depth):
        modl = mod[layer][:, None, :]
        j = layer // N_MIXERS
        if layer % N_MIXERS == 0:
            qkv, gt, ba, cb_new = _inproj_gdn_call(x, modl, wts["norm_pre_mix"][layer], p["w_gdn"], j,
                                                   wts["gdn_conv_w"][j], wts["gdn_conv_b"][j], s_gconv, nb, lb)
            og, new_gdn = _gdn_call(qkv, gt, ba, s_gdn, j, new_gdn, p["alog"][j], p["dtb"][j],
                                    wts["gdn_norm"][j], gdn_nch, nsq, lc)
            new_gconv.append(cb_new)
            w_out = p["gdn_w_out"]
        else:
            (proj,) = _inproj_call(x, modl, wts["norm_pre_mix"][layer], p["w_hgrn"], j, (4 * qk,), nb, lb)
            og, new_hgrn = _hgrn_call(proj, s_hgrn, j, new_hgrn, wts["hgrn_lb"], wts["hgrn_norm"][j],
                                      hgrn_nch, nsq, lc)
            w_out = p["hgrn_w_out"]
        x, fb_new = _ffn_call(og, x, modl, wts["norm_post_mix"][layer], wts["norm_pre_ffn"][layer],
                              wts["norm_post_ffn"][layer], w_out, j, p["wgu"],
                              wts["ffn_conv_w"][layer], wts["ffn_conv_b"][layer], p["wd"], layer, s_fconv, nb, lb)
        new_fconv.append(fb_new)
    return x, new_gdn, jnp.stack(new_gconv), new_hgrn, jnp.stack(new_fconv)


def kernel(x_prompt, x_sample, state_gdn, state_gdn_conv, state_hgrn, state_ffn_conv, c_prompt, c_sample, ada_w, ada_b, norm_pre_mix, norm_post_mix, norm_pre_ffn, norm_post_ffn, gdn_w_in, gdn_conv_w, gdn_conv_b, gdn_a_log, gdn_dt_bias, gdn_norm, gdn_w_out, hgrn_lb, hgrn_w_in, hgrn_norm, hgrn_w_out, ffn_w_gu, ffn_conv_w, ffn_conv_b, ffn_w_down):
    wts = dict(norm_pre_mix=norm_pre_mix, norm_post_mix=norm_post_mix, norm_pre_ffn=norm_pre_ffn,
               norm_post_ffn=norm_post_ffn, gdn_conv_w=gdn_conv_w, gdn_conv_b=gdn_conv_b, gdn_norm=gdn_norm,
               hgrn_lb=hgrn_lb, hgrn_norm=hgrn_norm, ffn_conv_w=ffn_conv_w, ffn_conv_b=ffn_conv_b)
    p = _prep_weights(gdn_w_in, gdn_a_log, gdn_dt_bias, hgrn_w_in, gdn_w_out, hgrn_w_out, ffn_w_gu, ffn_w_down)

    bp, sp, _ = x_prompt.shape
    bs, ss, _ = x_sample.shape
    mod = _mod_call(jnp.concatenate([c_prompt, c_sample], axis=0), ada_w, ada_b)
    mod_p, mod_s = mod[:, :bp], mod[:, bp:]

    dt = x_prompt.dtype
    zeros = lambda a: jnp.zeros((a.shape[0], bp) + a.shape[2:], dt)
    y_p, p_gdn, p_gconv, p_hgrn, p_fconv = _trunk(
        x_prompt, mod_p, zeros(state_gdn), zeros(state_gdn_conv), zeros(state_hgrn), zeros(state_ffn_conv),
        wts, p, nb=1, lb=min(TILE_ROWS, sp), nsq=1, lc=min(CHUNK, sp))
    y_s, s_gdn, s_gconv, s_hgrn, s_fconv = _trunk(
        x_sample, mod_s, state_gdn, state_gdn_conv, state_hgrn, state_ffn_conv,
        wts, p, nb=min(bs, SHORT_TILE_ROWS // ss), lb=ss, nsq=min(bs, CHUNK // ss), lc=ss)
    return (y_p, y_s, p_gdn, p_gconv, p_hgrn, p_fconv, s_gdn, s_gconv, s_hgrn, s_fconv)
```

```python
import functools

import numpy as np
import jax
import jax.numpy as jnp
from jax import lax
from jax.experimental import pallas as pl
from jax.experimental.pallas import tpu as pltpu

F32 = jnp.float32
BF16 = jnp.bfloat16
EPS = 1e-6
LB_FLOOR = 1e-30
N_MOD = 6
N_MIXERS = 2
HEADS = 8
HEAD_DIM = 128
GDN_CONV = 4
FFN_CONV = 3
CHUNK = 64
PAD_ROWS = 8
TILE_ROWS = 512
SHORT_TILE_ROWS = 256
GDN_CHUNKS = 8
HGRN_CHUNKS = 8
ROW_PARTS = 2
VMEM_LIMIT = 56 * 1024 * 1024
STATE_VMEM_BUDGET = 32 * 1024 * 1024


def _params(*sem):
    return pltpu.CompilerParams(dimension_semantics=sem, vmem_limit_bytes=VMEM_LIMIT)


def _sigmoid(x):
    return jax.nn.sigmoid(x)


def _silu(x):
    hx = 0.5 * x
    return hx + hx * jnp.tanh(hx)


def _softplus(x):
    return jnp.maximum(x, 0.0) + jnp.log1p(jnp.exp(-jnp.abs(x)))


def _rms(x, g):
    return x * lax.rsqrt(jnp.mean(x * x, axis=-1, keepdims=True) + EPS) * g


def _dot(a, b):
    return jnp.dot(a.astype(BF16), b.astype(BF16), preferred_element_type=F32)


def _bmm(a, b):
    return lax.dot_general(a.astype(BF16), b.astype(BF16), (((2,), (1,)), ((0,), (0,))),
                           preferred_element_type=F32)


def _bmm_nt(a, b):
    return lax.dot_general(a.astype(BF16), b.astype(BF16), (((2,), (2,)), ((0,), (0,))),
                           preferred_element_type=F32)


def _split2(x):
    hi = x.astype(BF16)
    lo = (x - hi.astype(F32)).astype(BF16)
    return hi, lo


def _pair_product(lhs, rhs, left):
    c = rhs.shape[1]
    n = len(lhs)
    rh, rl = _split2(rhs)
    bd = lambda x: jnp.concatenate([jnp.where(left, x, jnp.zeros_like(x)),
                                    jnp.where(left, jnp.zeros_like(x), x)], axis=1)
    pieces = [_split2(x) for x in lhs]
    his = [hi for hi, _ in pieces]
    los = [lo for _, lo in pieces]
    top = _bmm(jnp.concatenate(his + los, axis=1), bd(rh))
    bot = _bmm(his[0] if n == 1 else jnp.concatenate(his, axis=1), bd(rl))
    return [top[:, i * c:(i + 1) * c] + top[:, (n + i) * c:(n + i + 1) * c] + bot[:, i * c:(i + 1) * c]
            for i in range(n)]


def _dot_sel(sel, x, pieces):
    acc = None
    r = x
    for i in range(pieces):
        part = r.astype(BF16)
        if i + 1 < pieces:
            r = r - part.astype(F32)
        term = jnp.dot(sel, part, preferred_element_type=F32)
        acc = term if acc is None else acc + term
    return acc


def _idiv(x, p2):
    return x >> (p2.bit_length() - 1)


def _paired_unit_lower_inverse(a, block, span, left, row, col):
    eye = (col == row).astype(F32)
    if block >= span:
        return _paired_neumann(eye, a, span, left)
    diag = _idiv(row, block) == _idiv(col, block)
    inv = _paired_neumann(eye, jnp.where(diag, a, 0.0), block, left)
    (low,) = _pair_product([inv], jnp.where(diag, 0.0, a), left)
    blk = _paired_neumann(eye, low, span // block, left)
    (res,) = _pair_product([blk], inv, left)
    return res


def _paired_neumann(eye, d, order, left):
    p = eye - d
    if order <= 2:
        return p
    (dp,) = _pair_product([d], d, left)
    k = 4
    while k < order:
        pp, dp_next = _pair_product([p, dp], dp, left)
        p, dp = p + pp, dp_next
        k *= 2
    (pp,) = _pair_product([p], dp, left)
    return p + pp


def _items(x, nch, base, width):
    return jnp.concatenate(
        [x[ch * CHUNK:(ch + 1) * CHUNK, base + h * width:base + (h + 1) * width][None]
         for ch in range(nch) for h in range(HEADS)], axis=0)


def _mod_kernel(c_ref, w_ref, b_ref, o_ref):
    cs = _silu(c_ref[...])
    o_ref[0] = _dot(cs, w_ref[0]) + b_ref[0]


def _mod_call(c_all, ada_w, ada_b):
    depth, d, n = ada_w.shape
    rows = c_all.shape[0]
    tn = n // 4
    return pl.pallas_call(
        _mod_kernel,
        grid=(depth, n // tn),
        in_specs=[pl.BlockSpec((rows, d), lambda l, j: (0, 0)),
                  pl.BlockSpec((1, d, tn), lambda l, j: (l, 0, j)),
                  pl.BlockSpec((1, 1, tn), lambda l, j: (l, 0, j))],
        out_specs=pl.BlockSpec((1, rows, tn), lambda l, j: (l, 0, j)),
        out_shape=jax.ShapeDtypeStruct((depth, rows, n), F32),
        compiler_params=_params("arbitrary", "arbitrary"),
        name="adaln_mod",
    )(c_all, ada_w, ada_b.reshape(depth, 1, n))


def _row_parts(nb, lb, count):
    if nb >= count:
        return [(slice(i * nb // count, (i + 1) * nb // count), slice(0, lb)) for i in range(count)]
    return [(slice(0, nb), slice(i * lb // count, (i + 1) * lb // count)) for i in range(count)]


def _flat(v):
    return v.reshape(v.shape[0] * v.shape[1], v.shape[2])


def _inproj_kernel(x_ref, sh_ref, sc_ref, g_ref, w_ref, *out_refs, splits):
    nb, lb, d = x_ref.shape
    parts = _row_parts(nb, lb, ROW_PARTS)
    hbs = [_flat(_rms(x_ref[bs, rs, :], g_ref[...]) * (1.0 + sc_ref[bs]) + sh_ref[bs]).astype(BF16)
           for bs, rs in parts]
    off = 0
    for o_ref, n in zip(out_refs, splits):
        for (bs, rs), hb in zip(parts, hbs):
            res = jnp.dot(hb, w_ref[:, off:off + n], preferred_element_type=F32)
            o_ref[bs, rs, :] = res.reshape(bs.stop - bs.start, rs.stop - rs.start, n)
        off += n


def _inproj_gdn_kernel(x_ref, sh_ref, sc_ref, g_ref, w_ref, cw_ref, cb_ref, cbuf_ref,
                       qkv_ref, gt_ref, ba_ref, cbnew_ref, xbuf):
    nb, lb, d = x_ref.shape
    hd = HEAD_DIM
    qk = HEADS * hd
    w3 = 3 * qk
    t = pl.program_id(1)
    parts = _row_parts(nb, lb, ROW_PARTS)
    shape = lambda bs, rs: (bs.stop - bs.start, rs.stop - rs.start)

    @pl.when(t == 0)
    def _():
        xbuf[:, PAD_ROWS - (GDN_CONV - 1):PAD_ROWS, :] = cbuf_ref[...]

    @pl.when(t > 0)
    def _():
        xbuf[:, 0:PAD_ROWS, :] = xbuf[:, lb:lb + PAD_ROWS, :]

    hbs = [_flat(_rms(x_ref[bs, rs, :], g_ref[...]) * (1.0 + sc_ref[bs]) + sh_ref[bs]).astype(BF16)
           for bs, rs in parts]
    raws = [jnp.dot(hb, w_ref[:, 0:w3], preferred_element_type=F32) for hb in hbs]
    for (bs, rs), raw in zip(parts, raws):
        xbuf[bs, PAD_ROWS + rs.start:PAD_ROWS + rs.stop, :] = raw.reshape(shape(bs, rs) + (w3,))
    cbnew_ref[...] = xbuf[:, lb + PAD_ROWS - (GDN_CONV - 1):lb + PAD_ROWS, :]

    cw = cw_ref[...]
    for (bs, rs), hb, raw in zip(parts, hbs, raws):
        pb, pr = shape(bs, rs)
        y = cb_ref[...] + raw.reshape(pb, pr, w3) * cw[GDN_CONV - 1:GDN_CONV]
        for j in range(GDN_CONV - 1):
            s = PAD_ROWS - (GDN_CONV - 1) + j
            y = y + xbuf[bs, s + rs.start:s + rs.stop, :] * cw[j:j + 1]
        a = _silu(y)
        for h in range(2 * HEADS):
            hs = slice(h * hd, (h + 1) * hd)
            v = a[:, :, hs]
            scale = lax.rsqrt(jnp.sum(v * v, axis=-1, keepdims=True) + EPS)
            qkv_ref[bs, rs, hs] = v * (scale * (hd ** -0.5) if h < HEADS else scale)
        qkv_ref[bs, rs, 2 * qk:w3] = a[:, :, 2 * qk:w3]
        gt_ref[bs, rs, :] = jnp.dot(hb, w_ref[:, w3:w3 + qk], preferred_element_type=F32).reshape(pb, pr, qk)
        ba_ref[bs, rs, :] = jnp.dot(hb, w_ref[:, w3 + qk:], preferred_element_type=F32).reshape(pb, pr, hd)


def _inproj_gdn_call(x, mod, gain, w, j, cw, cb, cbuf, nb, lb):
    bsz, seq, d = x.shape
    ntot = w.shape[2]
    qk = HEADS * HEAD_DIM
    w3 = 3 * qk
    row = lambda k: pl.BlockSpec((nb, 1, d), lambda b, t, k=k: (b, 0, k))
    tok = lambda n: pl.BlockSpec((nb, lb, n), lambda b, t: (b, t, 0))
    taps = GDN_CONV - 1
    return pl.pallas_call(
        _inproj_gdn_kernel,
        grid=(bsz // nb, seq // lb),
        in_specs=[tok(d), row(0), row(1),
                  pl.BlockSpec((1, d), lambda b, t: (0, 0)),
                  _layer_weight(d, ntot, j),
                  pl.BlockSpec((GDN_CONV, w3), lambda b, t: (0, 0)),
                  pl.BlockSpec((1, w3), lambda b, t: (0, 0)),
                  pl.BlockSpec((None, nb, taps, w3), lambda b, t: (j, b, 0, 0))],
        out_specs=[tok(w3), tok(qk), tok(HEAD_DIM), pl.BlockSpec((nb, taps, w3), lambda b, t: (b, 0, 0))],
        out_shape=[jax.ShapeDtypeStruct((bsz, seq, w3), F32), jax.ShapeDtypeStruct((bsz, seq, qk), F32),
                   jax.ShapeDtypeStruct((bsz, seq, HEAD_DIM), F32),
                   jax.ShapeDtypeStruct((bsz, taps, w3), F32)],
        scratch_shapes=[pltpu.VMEM((nb, lb + PAD_ROWS, w3), F32)],
        compiler_params=_params("arbitrary", "arbitrary"),
        name="inproj_gdn",
    )(x, mod, mod, gain.reshape(1, d), w, cw, cb.reshape(1, w3), cbuf)


def _inproj_call(x, mod, gain, w, j, splits, nb, lb):
    bsz, seq, d = x.shape
    ntot = w.shape[2]
    row = lambda k: pl.BlockSpec((nb, 1, d), lambda b, t, k=k: (b, 0, k))
    return pl.pallas_call(
        functools.partial(_inproj_kernel, splits=splits),
        grid=(bsz // nb, seq // lb),
        in_specs=[pl.BlockSpec((nb, lb, d), lambda b, t: (b, t, 0)),
                  row(0), row(1),
                  pl.BlockSpec((1, d), lambda b, t: (0, 0)),
                  _layer_weight(d, ntot, j)],
        out_specs=[pl.BlockSpec((nb, lb, n), lambda b, t: (b, t, 0)) for n in splits],
        out_shape=[jax.ShapeDtypeStruct((bsz, seq, n), F32) for n in splits],
        compiler_params=_params("arbitrary", "arbitrary"),
        name="inproj",
    )(x, mod, mod, gain.reshape(1, d), w)


def _ffn_kernel(og_ref, x_ref, g1_ref, sh2_ref, sc2_ref, g2_ref, npost_ref, npre_ref, npostf_ref,
                wout_ref, wg_ref, wu_ref, cw_ref, cb_ref, wd_ref, fbuf_ref,
                xo_ref, fbnew_ref, gbuf):
    nb, lb, d = x_ref.shape
    f = wg_ref.shape[1]
    t = pl.program_id(1)
    parts = _row_parts(nb, lb, ROW_PARTS)
    shape = lambda bs, rs: (bs.stop - bs.start, rs.stop - rs.start)
    flat = _flat

    @pl.when(t == 0)
    def _():
        gbuf[:, PAD_ROWS - (FFN_CONV - 1):PAD_ROWS, :] = fbuf_ref[...]

    @pl.when(t > 0)
    def _():
        gbuf[:, 0:PAD_ROWS, :] = gbuf[:, lb:lb + PAD_ROWS, :]

    mix = [jnp.dot(flat(og_ref[bs, rs, :]).astype(BF16), wout_ref[...], preferred_element_type=F32)
           for bs, rs in parts]
    x1, hb = [], []
    for (bs, rs), mx in zip(parts, mix):
        pb, pr = shape(bs, rs)
        xa = x_ref[bs, rs, :] + (1.0 + g1_ref[bs]) * _rms(mx.reshape(pb, pr, d), npost_ref[...])
        x1.append(xa)
        hb.append(flat(_rms(xa, npre_ref[...]) * (1.0 + sc2_ref[bs]) + sh2_ref[bs]).astype(BF16))
    gt = [jnp.dot(h, wg_ref[...], preferred_element_type=F32) for h in hb]
    up = [jnp.dot(h, wu_ref[...], preferred_element_type=F32) for h in hb]
    for (bs, rs), g in zip(parts, gt):
        pb, pr = shape(bs, rs)
        gbuf[bs, PAD_ROWS + rs.start:PAD_ROWS + rs.stop, :] = g.reshape(pb, pr, f)
    fbnew_ref[...] = gbuf[:, lb + PAD_ROWS - (FFN_CONV - 1):lb + PAD_ROWS, :]

    cw = cw_ref[...]
    out = []
    for (bs, rs), g, u in zip(parts, gt, up):
        pb, pr = shape(bs, rs)
        y = cb_ref[...] + g.reshape(pb, pr, f) * cw[FFN_CONV - 1:FFN_CONV]
        for j in range(FFN_CONV - 1):
            s = PAD_ROWS - (FFN_CONV - 1) + j
            y = y + gbuf[bs, s + rs.start:s + rs.stop, :] * cw[j:j + 1]
        act = flat(_silu(y)) * u
        out.append(jnp.dot(act.astype(BF16), wd_ref[...], preferred_element_type=F32))
    for (bs, rs), xa, o in zip(parts, x1, out):
        pb, pr = shape(bs, rs)
        xo_ref[bs, rs, :] = xa + (1.0 + g2_ref[bs]) * _rms(o.reshape(pb, pr, d), npostf_ref[...])


def _layer_weight(rows, cols, layer, col_block=0):
    return pl.BlockSpec((None, rows, cols), lambda b, t: (layer, 0, col_block), pipeline_mode=pl.Buffered(1))


def _ffn_call(og, x, mod, npost, npre, npostf, wout, jw, wgu, cw, cb, wd, layer, fbuf, nb, lb):
    bsz, seq, d = x.shape
    f = wd.shape[1]
    row = lambda k: pl.BlockSpec((nb, 1, d), lambda b, t, k=k: (b, 0, k))
    vec = lambda n: pl.BlockSpec((1, n), lambda b, t: (0, 0))
    tok = pl.BlockSpec((nb, lb, d), lambda b, t: (b, t, 0))
    taps = FFN_CONV - 1
    return pl.pallas_call(
        _ffn_kernel,
        grid=(bsz // nb, seq // lb),
        in_specs=[tok, tok, row(2), row(3), row(4), row(5), vec(d), vec(d), vec(d),
                  _layer_weight(d, d, jw), _layer_weight(d, f, layer, 0), _layer_weight(d, f, layer, 1),
                  pl.BlockSpec((FFN_CONV, f), lambda b, t: (0, 0)), vec(f), _layer_weight(f, d, layer),
                  pl.BlockSpec((None, nb, taps, f), lambda b, t: (layer, b, 0, 0))],
        out_specs=[tok, pl.BlockSpec((nb, taps, f), lambda b, t: (b, 0, 0))],
        out_shape=[jax.ShapeDtypeStruct((bsz, seq, d), F32),
                   jax.ShapeDtypeStruct((bsz, taps, f), F32)],
        scratch_shapes=[pltpu.VMEM((nb, lb + PAD_ROWS, f), F32)],
        compiler_params=_params("arbitrary", "arbitrary"),
        name="ffn_block",
    )(og, x, mod, mod, mod, mod, npost.reshape(1, d), npre.reshape(1, d), npostf.reshape(1, d),
      wout, wgu, wgu, cw, cb.reshape(1, f), wd, fbuf)


def _seq_masks(nsq, lc):
    c = nsq * lc
    i = np.arange(c)[:, None]
    j = np.arange(c)[None, :]
    same = (i // lc) == (j // lc)
    return i, j, same


def _gdn_consts(nsq, lc):
    i, j, same = _seq_masks(nsq, lc)
    return jnp.asarray((same & (j <= i)).astype(np.float32), BF16)


def _hgrn_levels(lc):
    s, out = lc, []
    while s >= 4:
        out.append(s)
        s //= 2
    return out


def _hgrn_consts(nsq, lc):
    i, j, same = _seq_masks(nsq, lc)
    mats = [same & (j <= i), same & (j > i)]
    for s in _hgrn_levels(lc):
        mid = (i // s) * s + s // 2
        mats.append(np.where(i >= mid, (j >= mid) & (j <= i), (j > i) & (j < mid)))
    return jnp.asarray(np.concatenate(mats, axis=0).astype(np.float32), BF16)


def _state_index(ch, n, nsq, lc, lr):
    return ((ch * nsq + n) * lc) // lr


def _store_heads(og_ref, ch, nsq, lc, val):
    for h in range(HEADS):
        hs = slice(h * HEAD_DIM, (h + 1) * HEAD_DIM)
        if og_ref.shape[0] == 1:
            og_ref[0, ch * CHUNK:(ch + 1) * CHUNK, hs] = val[h]
        else:
            og_ref[ch * nsq:(ch + 1) * nsq, :, hs] = val[h].reshape(nsq, lc, HEAD_DIM)


def _gdn_kernel(qkv_ref, gt_ref, ba_ref, s0_ref, alog_ref, dtb_ref, ng_ref, tri_ref, *rest, nch, nsq, lc):
    og_ref, snew_ref, s_scr = rest[-3:]
    ns, lr, w3 = qkv_ref.shape
    r = ns * lr
    hd = HEAD_DIM
    qk = HEADS * hd
    t = pl.program_id(1)
    last = pl.num_programs(1) - 1

    @pl.when(t == 0)
    def _():
        s_scr[...] = s0_ref[...]

    qkv = qkv_ref[...].reshape(r, w3)
    gt = gt_ref[...].reshape(r, qk)

    ba = ba_ref[...].reshape(r, hd)
    beta = _sigmoid(ba)
    g = -jnp.exp(alog_ref[...]) * _softplus(ba + dtb_ref[...])
    chunks = [slice(ch * CHUNK, (ch + 1) * CHUNK) for ch in range(nch)]
    gcum = [_dot_sel(tri_ref[...], g[sl], 3) for sl in chunks]
    per_item = lambda f: jnp.concatenate([f(ch, h)[None] for ch in range(nch) for h in range(HEADS)], axis=0)
    gc = per_item(lambda ch, h: gcum[ch][:, HEADS + h:HEADS + h + 1])
    bcol = per_item(lambda ch, h: beta[chunks[ch], h:h + 1])

    q = _items(qkv, nch, 0, hd)
    k = _items(qkv, nch, qk, hd)
    v = _items(qkv, nch, 2 * qk, hd)
    eg = jnp.exp(gc)
    kb = k * bcol
    qg = q * eg
    rhs = jnp.concatenate([v * bcol, kb * eg], axis=-1).astype(BF16)
    gate = _silu(_items(gt, nch, 0, hd))
    rid = _idiv(lax.broadcasted_iota(jnp.int32, (1, CHUNK, 1), 1), lc)

    even = lambda x: jnp.concatenate([x[i:i + 1] for i in range(0, x.shape[0], 2)], axis=0)
    odd = lambda x: jnp.concatenate([x[i:i + 1] for i in range(1, x.shape[0], 2)], axis=0)
    side = lambda x: jnp.concatenate([even(x), odd(x)], axis=-1)
    stack = lambda x: jnp.concatenate(
        [jnp.concatenate([even(x), jnp.zeros_like(even(x))], axis=-1),
         jnp.concatenate([jnp.zeros_like(odd(x)), odd(x)], axis=-1)], axis=1)
    lane = lax.broadcasted_iota(jnp.int32, (1, 1, 2 * CHUNK), 2)
    left = lane < CHUNK
    col = lane & (CHUNK - 1)
    row = lax.broadcasted_iota(jnp.int32, (1, CHUNK, 1), 1)
    same = _idiv(row, lc) == _idiv(col, lc)
    incl = same & (col <= row)
    strict = same & (col < row)
    gcp = jnp.where(left, even(gc), odd(gc))
    grp = jnp.sum(jnp.where(col == row, gcp, 0.0), axis=1, keepdims=True)
    decay = jnp.where(incl, jnp.exp(jnp.where(incl, gcp - grp, 0.0)), 0.0)
    kst = stack(k.astype(BF16))
    a = jnp.where(strict, _bmm_nt(side(kb), kst) * decay, 0.0)
    attn = jnp.where(incl, _bmm_nt(side(q), kst) * decay, 0.0)
    tinv = _paired_unit_lower_inverse(a, min(16, lc), lc, left, row, col)
    th, tl = _split2(tinv)
    rst = stack(rhs)
    sol = _bmm(th, rst) + _bmm(tl, rst)
    heads_of = lambda x, off: jnp.concatenate(
        [x[p:p + 1, :, s * (x.shape[-1] // 2) + off:s * (x.shape[-1] // 2) + off + hd]
         for p in range(x.shape[0]) for s in range(2)], axis=0)
    uv, w = heads_of(sol, 0), heads_of(sol, hd)

    for ch in range(nch):
        it = slice(ch * HEADS, (ch + 1) * HEADS)
        pr = slice(ch * HEADS // 2, (ch + 1) * HEADS // 2)
        us, os_ = [], []
        for n in range(nsq):
            sl = slice(n * lc, (n + 1) * lc)
            st = s_scr[_state_index(ch, n, nsq, lc, lr)]
            ws = _bmm(jnp.concatenate([w[it, sl], qg[it, sl]], axis=1), st)
            us.append(uv[it, sl] - ws[:, :lc])
            os_.append(ws[:, lc:])
        u = us[0] if nsq == 1 else jnp.concatenate(us, axis=1)
        o = (os_[0] if nsq == 1 else jnp.concatenate(os_, axis=1)) + heads_of(_bmm(attn[pr], stack(u.astype(BF16))), 0)
        for n in range(nsq):
            si = _state_index(ch, n, nsq, lc, lr)
            gl = gc[it, (n + 1) * lc - 1:(n + 1) * lc, :]
            if nsq > 1:
                kd = jnp.where(rid == n, k[it] * jnp.exp(jnp.where(rid == n, gl - gc[it], 0.0)), 0.0)
            else:
                kd = k[it] * jnp.exp(gl - gc[it])
            s_scr[si] = s_scr[si] * jnp.exp(gl) + _bmm(jnp.swapaxes(kd, 1, 2), u)
        _store_heads(og_ref, ch, nsq, lc, _rms(o, ng_ref[...]) * gate[it])

    @pl.when(t == last)
    def _():
        for n in range(ns):
            _write_state(snew_ref, n, s_scr[n])


def _stacked_state(states, j, prev, ns):
    tail = states.shape[2:]
    in_spec = pl.BlockSpec((None, ns) + tail, lambda b, t: (j, b, 0, 0, 0))
    if prev is None:
        out_spec = pl.BlockSpec((states.shape[0], ns) + tail, lambda b, t: (0, b, 0, 0, 0))
        extra_in, extra_specs = [], []
    else:
        out_spec = in_spec
        extra_in, extra_specs = [prev], [pl.BlockSpec(memory_space=pl.ANY)]
    return in_spec, out_spec, extra_in, extra_specs, jax.ShapeDtypeStruct(states.shape, states.dtype)


def _write_state(snew_ref, n, value):
    if len(snew_ref.shape) == 5:
        for m in range(snew_ref.shape[0]):
            snew_ref[m, n] = value
    else:
        snew_ref[n] = value


def _gdn_call(qkv, gt, ba, states, j, prev, alog, dtb, ng, nch, nsq, lc):
    bsz, seq, w3 = qkv.shape
    qk = w3 // 3
    hd = HEAD_DIM
    tri = _gdn_consts(nsq, lc)
    ns, lr = (1, nch * CHUNK) if nsq == 1 else (nch * nsq, lc)
    tok = lambda n: pl.BlockSpec((ns, lr, n), lambda b, t: (b, t, 0))
    vec = lambda n: pl.BlockSpec((1, n), lambda b, t: (0, 0))
    state, state_out, extra_in, extra_specs, state_shape = _stacked_state(states, j, prev, ns)
    inputs = [qkv, gt, ba, states, alog, dtb, ng.reshape(1, hd), tri] + extra_in
    return pl.pallas_call(
        functools.partial(_gdn_kernel, nch=nch, nsq=nsq, lc=lc),
        grid=(bsz // ns, seq // lr),
        in_specs=[tok(w3), tok(qk), tok(hd), state, vec(hd), vec(hd), vec(hd),
                  pl.BlockSpec((CHUNK, CHUNK), lambda b, t: (0, 0))] + extra_specs,
        out_specs=[tok(qk), state_out],
        out_shape=[jax.ShapeDtypeStruct((bsz, seq, qk), F32), state_shape],
        input_output_aliases={len(inputs) - 1: 1} if extra_in else {},
        scratch_shapes=[pltpu.VMEM((ns, HEADS, hd, hd), F32)],
        compiler_params=_params("arbitrary", "arbitrary"),
        name="gdn_core",
    )(*inputs)


def _hgrn_kernel(proj_ref, s0_ref, lbraw_ref, ng_ref, cm_ref, *rest, nch, nsq, lc, layer):
    og_ref, snew_ref, st_scr = rest[-3:]
    ns, lr, w4 = proj_ref.shape
    r = ns * lr
    hd = HEAD_DIM
    qf = HEADS * hd
    t = pl.program_id(1)
    last = pl.num_programs(1) - 1

    @pl.when(t == 0)
    def _():
        for n in range(ns):
            st_scr[n] = jnp.swapaxes(s0_ref[n], 1, 2)

    lbraw = lbraw_ref[...]
    e = jnp.exp(lbraw - jnp.max(lbraw, axis=0, keepdims=True))
    p = e / jnp.sum(e, axis=0, keepdims=True)
    lb = jnp.zeros((1, qf), F32)
    for m in range(1, layer + 1):
        lb = lb + p[m:m + 1, :]

    proj = proj_ref[...].reshape(r, w4)
    fr = proj[:, qf:2 * qf]
    efr = jnp.exp(-jnp.abs(fr))
    big = 1.0 / (1.0 + efr)
    small = efr * big
    sig_pos = jnp.where(fr >= 0.0, big, small)
    sig_neg = jnp.where(fr >= 0.0, small, big)
    forget = jnp.maximum(lb, LB_FLOOR) + (1.0 - lb) * sig_pos
    logf = jnp.log(forget)

    levels = _hgrn_levels(lc)
    nmat = 2 + len(levels)
    cm = cm_ref[...]
    ex = jnp.concatenate([jnp.exp(_dot_sel(cm, logf[ch * CHUNK:(ch + 1) * CHUNK], 2)) for ch in range(nch)], axis=0)
    factor = lambda m: jnp.concatenate(
        [ex[(ch * nmat + m) * CHUNK:(ch * nmat + m + 1) * CHUNK, h * hd:(h + 1) * hd][None]
         for ch in range(nch) for h in range(HEADS)], axis=0)

    q = _silu(_items(proj, nch, 0, hd))
    k = _items((1.0 - lb) * sig_neg, nch, 0, hd)
    v = _items(proj, nch, 2 * qf, hd)
    gate = _silu(_items(proj, nch, 3 * qf, hd))

    row = lax.broadcasted_iota(jnp.int32, (CHUNK, CHUNK), 0)
    col = lax.broadcasted_iota(jnp.int32, (CHUNK, CHUNK), 1)
    qb, kb = q.astype(BF16), k.astype(BF16)
    attn = jnp.where((row == col)[None], _bmm_nt(qb, kb), 0.0)
    odd = (lax.broadcasted_iota(jnp.int32, (1, CHUNK, 1), 1) & 1) == 1
    factors = [jnp.where(odd, _items(forget, nch, 0, hd), 1.0)] + [factor(2 + li) for li in range(len(levels))]
    for s, fl in zip([2] + levels, factors):
        mask = (_idiv(row, s) == _idiv(col, s)) & ((row & (s - 1)) >= s // 2) & ((col & (s - 1)) < s // 2)
        flb = fl.astype(BF16)
        attn = attn + jnp.where(mask[None], _bmm_nt(qb * flb, kb * flb), 0.0)
    eg = factor(0)
    qe = q * eg
    kd = k * factor(1)
    cid = _idiv(lax.broadcasted_iota(jnp.int32, (1, 1, CHUNK), 2), lc)

    for ch in range(nch):
        it = slice(ch * HEADS, (ch + 1) * HEADS)
        vt = jnp.swapaxes(v[it], 1, 2)
        os_ = []
        for n in range(nsq):
            sl = slice(n * lc, (n + 1) * lc)
            si = _state_index(ch, n, nsq, lc, lr)
            st = st_scr[si]
            os_.append(_bmm_nt(qe[it, sl], st))
            vtn = vt if nsq == 1 else jnp.where(cid == n, vt, 0.0)
            egl = eg[it, (n + 1) * lc - 1:(n + 1) * lc, :]
            st_scr[si] = st * egl + _bmm(vtn, kd[it])
        o = (os_[0] if nsq == 1 else jnp.concatenate(os_, axis=1)) + _bmm(attn[it], v[it])
        _store_heads(og_ref, ch, nsq, lc, _rms(o, ng_ref[...]) * gate[it])

    @pl.when(t == last)
    def _():
        for n in range(ns):
            _write_state(snew_ref, n, jnp.swapaxes(st_scr[n], 1, 2))


def _hgrn_call(proj, states, j, prev, lbraw, ng, nch, nsq, lc):
    bsz, seq, w4 = proj.shape
    qf = w4 // 4
    hd = HEAD_DIM
    cm = _hgrn_consts(nsq, lc)
    ns, lr = (1, nch * CHUNK) if nsq == 1 else (nch * nsq, lc)
    tok = lambda n: pl.BlockSpec((ns, lr, n), lambda b, t: (b, t, 0))
    state, state_out, extra_in, extra_specs, state_shape = _stacked_state(states, j, prev, ns)
    inputs = [proj, states, lbraw, ng.reshape(1, hd), cm] + extra_in
    return pl.pallas_call(
        functools.partial(_hgrn_kernel, nch=nch, nsq=nsq, lc=lc, layer=j),
        grid=(bsz // ns, seq // lr),
        in_specs=[tok(w4), state,
                  pl.BlockSpec(lbraw.shape, lambda b, t: (0, 0)),
                  pl.BlockSpec((1, hd), lambda b, t: (0, 0)),
                  pl.BlockSpec(cm.shape, lambda b, t: (0, 0))] + extra_specs,
        out_specs=[tok(qf), state_out],
        out_shape=[jax.ShapeDtypeStruct((bsz, seq, qf), F32), state_shape],
        input_output_aliases={len(inputs) - 1: 1} if extra_in else {},
        scratch_shapes=[pltpu.VMEM((ns, HEADS, hd, hd), F32)],
        compiler_params=_params("arbitrary", "arbitrary"),
        name="hgrn_core",
    )(*inputs)


def _prep_weights(gdn_w_in, gdn_a_log, gdn_dt_bias, hgrn_w_in, gdn_w_out, hgrn_w_out, ffn_w_gu, ffn_w_down):
    pad = HEAD_DIM - 2 * HEADS
    w_gdn = jnp.pad(gdn_w_in, ((0, 0), (0, 0), (0, pad))).astype(BF16)
    gate_pad = lambda a: jnp.pad(a, ((0, 0), (HEADS, HEAD_DIM - 2 * HEADS)))[:, None, :]
    return dict(
        w_gdn=w_gdn, alog=gate_pad(gdn_a_log), dtb=gate_pad(gdn_dt_bias),
        w_hgrn=hgrn_w_in.astype(BF16), gdn_w_out=gdn_w_out.astype(BF16), hgrn_w_out=hgrn_w_out.astype(BF16),
        wgu=ffn_w_gu.astype(BF16), wd=ffn_w_down.astype(BF16))


def _trunk(x, mod, s_gdn, s_gconv, s_hgrn, s_fconv, wts, p, nb, lb, nsq, lc):
    depth = mod.shape[0]
    bsz, seq, _ = x.shape
    qk = HEADS * HEAD_DIM
    if nsq == 1:
        chunks_avail = seq // CHUNK
    else:
        state_bytes = (3 + 2 * max(s_gdn.shape[0], s_hgrn.shape[0])) * nsq * HEADS * HEAD_DIM * HEAD_DIM * 4
        chunks_avail = max(1, min(bsz // nsq, STATE_VMEM_BUDGET // state_bytes))
    gdn_nch = min(GDN_CHUNKS, chunks_avail)
    hgrn_nch = min(HGRN_CHUNKS, chunks_avail)
    new_gdn, new_gconv, new_hgrn, new_fconv = None, [], None, []
    for layer in range(depth):
        modl = mod[layer][:, None, :]
        j = layer // N_MIXERS
        if layer % N_MIXERS == 0:
            qkv, gt, ba, cb_new = _inproj_gdn_call(x, modl, wts["norm_pre_mix"][layer], p["w_gdn"], j,
                                                   wts["gdn_conv_w"][j], wts["gdn_conv_b"][j], s_gconv, nb, lb)
            og, new_gdn = _gdn_call(qkv, gt, ba, s_gdn, j, new_gdn, p["alog"][j], p["dtb"][j],
                                    wts["gdn_norm"][j], gdn_nch, nsq, lc)
            new_gconv.append(cb_new)
            w_out = p["gdn_w_out"]
        else:
            (proj,) = _inproj_call(x, modl, wts["norm_pre_mix"][layer], p["w_hgrn"], j, (4 * qk,), nb, lb)
            og, new_hgrn = _hgrn_call(proj, s_hgrn, j, new_hgrn, wts["hgrn_lb"], wts["hgrn_norm"][j],
                                      hgrn_nch, nsq, lc)
            w_out = p["hgrn_w_out"]
        x, fb_new = _ffn_call(og, x, modl, wts["norm_post_mix"][layer], wts["norm_pre_ffn"][layer],
                              wts["norm_post_ffn"][layer], w_out, j, p["wgu"],
                              wts["ffn_conv_w"][layer], wts["ffn_conv_b"][layer], p["wd"], layer, s_fconv, nb, lb)
        new_fconv.append(fb_new)
    return x, new_gdn, jnp.stack(new_gconv), new_hgrn, jnp.stack(new_fconv)


def kernel(x_prompt, x_sample, state_gdn, state_gdn_conv, state_hgrn, state_ffn_conv, c_prompt, c_sample, ada_w, ada_b, norm_pre_mix, norm_post_mix, norm_pre_ffn, norm_post_ffn, gdn_w_in, gdn_conv_w, gdn_conv_b, gdn_a_log, gdn_dt_bias, gdn_norm, gdn_w_out, hgrn_lb, hgrn_w_in, hgrn_norm, hgrn_w_out, ffn_w_gu, ffn_conv_w, ffn_conv_b, ffn_w_down):
    wts = dict(norm_pre_mix=norm_pre_mix, norm_post_mix=norm_post_mix, norm_pre_ffn=norm_pre_ffn,
               norm_post_ffn=norm_post_ffn, gdn_conv_w=gdn_conv_w, gdn_conv_b=gdn_conv_b, gdn_norm=gdn_norm,
               hgrn_lb=hgrn_lb, hgrn_norm=hgrn_norm, ffn_conv_w=ffn_conv_w, ffn_conv_b=ffn_conv_b)
    p = _prep_weights(gdn_w_in, gdn_a_log, gdn_dt_bias, hgrn_w_in, gdn_w_out, hgrn_w_out, ffn_w_gu, ffn_w_down)

    bp, sp, _ = x_prompt.shape
    bs, ss, _ = x_sample.shape
    mod = _mod_call(jnp.concatenate([c_prompt, c_sample], axis=0), ada_w, ada_b)
    mod_p, mod_s = mod[:, :bp], mod[:, bp:]

    dt = x_prompt.dtype
    zeros = lambda a: jnp.zeros((a.shape[0], bp) + a.shape[2:], dt)
    y_p, p_gdn, p_gconv, p_hgrn, p_fconv = _trunk(
        x_prompt, mod_p, zeros(state_gdn), zeros(state_gdn_conv), zeros(state_hgrn), zeros(state_ffn_conv),
        wts, p, nb=1, lb=min(TILE_ROWS, sp), nsq=1, lc=min(CHUNK, sp))
    y_s, s_gdn, s_gconv, s_hgrn, s_fconv = _trunk(
        x_sample, mod_s, state_gdn, state_gdn_conv, state_hgrn, state_ffn_conv,
        wts, p, nb=min(bs, SHORT_TILE_ROWS // ss), lb=ss, nsq=min(bs, CHUNK // ss), lc=ss)
    return (y_p, y_s, p_gdn, p_gconv, p_hgrn, p_fconv, s_gdn, s_gconv, s_hgrn, s_fconv)
```

```python
import functools

import numpy as np
import jax
import jax.numpy as jnp
from jax import lax
from jax.experimental import pallas as pl
from jax.experimental.pallas import tpu as pltpu

F32 = jnp.float32
BF16 = jnp.bfloat16
EPS = 1e-6
LB_FLOOR = 1e-30
N_MIXERS = 2
LANES = 128
HEADS = 8
HEAD_DIM = 128
GDN_CONV = 4
FFN_CONV = 3
CHUNK = 64
PAD_ROWS = 8
TILE_ROWS = 512
SHORT_TILE_ROWS = 256
GDN_CHUNKS = 8
HGRN_CHUNKS = 8
ROW_PARTS = 2
VMEM_LIMIT = 56 * 1024 * 1024
STATE_VMEM_BUDGET = 32 * 1024 * 1024


def _params(*sem):
    return pltpu.CompilerParams(dimension_semantics=sem, vmem_limit_bytes=VMEM_LIMIT)


def _sigmoid(x):
    return jax.nn.sigmoid(x)


def _silu(x):
    hx = 0.5 * x
    return hx + hx * jnp.tanh(hx)


def _softplus(x):
    return jnp.maximum(x, 0.0) + jnp.log1p(jnp.exp(-jnp.abs(x)))


def _rms(x, g):
    return x * lax.rsqrt(jnp.mean(x * x, axis=-1, keepdims=True) + EPS) * g


def _dot(a, b):
    return jnp.dot(a.astype(BF16), b.astype(BF16), preferred_element_type=F32)


def _bmm(a, b):
    return lax.dot_general(a.astype(BF16), b.astype(BF16), (((2,), (1,)), ((0,), (0,))),
                           preferred_element_type=F32)


def _bmm_nt(a, b):
    return lax.dot_general(a.astype(BF16), b.astype(BF16), (((2,), (2,)), ((0,), (0,))),
                           preferred_element_type=F32)


def _split2(x):
    hi = x.astype(BF16)
    lo = (x - hi.astype(F32)).astype(BF16)
    return hi, lo


def _even(x):
    return jnp.concatenate([x[i:i + 1] for i in range(0, x.shape[0], 2)], axis=0)


def _odd(x):
    return jnp.concatenate([x[i:i + 1] for i in range(1, x.shape[0], 2)], axis=0)


def _side(x):
    return jnp.concatenate([_even(x), _odd(x)], axis=-1)


def _stack(x):
    e, o = _even(x), _odd(x)
    return jnp.concatenate([jnp.concatenate([e, jnp.zeros_like(e)], axis=-1),
                            jnp.concatenate([jnp.zeros_like(o), o], axis=-1)], axis=1)


def _halves(x, off, width):
    half = x.shape[-1] // 2
    return jnp.concatenate([x[p:p + 1, :, s * half + off:s * half + off + width]
                            for p in range(x.shape[0]) for s in range(2)], axis=0)


def _pair_product(lhs, rhs, grp):
    c = rhs.shape[1]
    n = len(lhs)
    bd = lambda x: jnp.concatenate([jnp.where(grp == m, x, jnp.zeros_like(x))
                                    for m in range(x.shape[-1] // c)], axis=1)
    rh, rl = _split2(rhs)
    pieces = [_split2(x) for x in lhs]
    his = [hi for hi, _ in pieces]
    los = [lo for _, lo in pieces]
    top = _bmm(jnp.concatenate(his + los, axis=1), bd(rh))
    bot = _bmm(his[0] if n == 1 else jnp.concatenate(his, axis=1), bd(rl))
    return [top[:, i * c:(i + 1) * c] + top[:, (n + i) * c:(n + i + 1) * c] + bot[:, i * c:(i + 1) * c]
            for i in range(n)]


def _dot_sel(sel, x, pieces):
    acc = None
    r = x
    for i in range(pieces):
        part = r.astype(BF16)
        if i + 1 < pieces:
            r = r - part.astype(F32)
        term = jnp.dot(sel, part, preferred_element_type=F32)
        acc = term if acc is None else acc + term
    return acc


def _idiv(x, p2):
    return x >> (p2.bit_length() - 1)


def _packed_unit_lower_inverse(a, block, span, grp, row, col):
    eye = (col == row).astype(F32)
    if block >= span:
        return _packed_neumann(eye, a, span, grp)
    diag = _idiv(row, block) == _idiv(col, block)
    inv = _packed_neumann(eye, jnp.where(diag, a, 0.0), block, grp)
    (low,) = _pair_product([inv], jnp.where(diag, 0.0, a), grp)
    blk = _packed_neumann(eye, low, span // block, grp)
    (res,) = _pair_product([blk], inv, grp)
    return res


def _packed_neumann(eye, d, order, grp):
    p = eye - d
    if order <= 2:
        return p
    (dp,) = _pair_product([d], d, grp)
    k = 4
    while k < order:
        pp, dp_next = _pair_product([p, dp], dp, grp)
        p, dp = p + pp, dp_next
        k *= 2
    (pp,) = _pair_product([p], dp, grp)
    return p + pp


def _items(x, nch, base, width):
    return jnp.concatenate(
        [x[ch * CHUNK:(ch + 1) * CHUNK, base + h * width:base + (h + 1) * width][None]
         for ch in range(nch) for h in range(HEADS)], axis=0)


def _mod_kernel(c_ref, w_ref, b_ref, o_ref):
    cs = _silu(c_ref[...])
    o_ref[0] = _dot(cs, w_ref[0]) + b_ref[0]


def _mod_call(c_all, ada_w, ada_b):
    depth, d, n = ada_w.shape
    rows = c_all.shape[0]
    tn = n // 4
    return pl.pallas_call(
        _mod_kernel,
        grid=(depth, n // tn),
        in_specs=[pl.BlockSpec((rows, d), lambda l, j: (0, 0)),
                  pl.BlockSpec((1, d, tn), lambda l, j: (l, 0, j)),
                  pl.BlockSpec((1, 1, tn), lambda l, j: (l, 0, j))],
        out_specs=pl.BlockSpec((1, rows, tn), lambda l, j: (l, 0, j)),
        out_shape=jax.ShapeDtypeStruct((depth, rows, n), F32),
        compiler_params=_params("arbitrary", "arbitrary"),
        name="adaln_mod",
    )(c_all, ada_w, ada_b.reshape(depth, 1, n))


def _row_parts(nb, lb, count):
    if nb >= count:
        return [(slice(i * nb // count, (i + 1) * nb // count), slice(0, lb)) for i in range(count)]
    return [(slice(0, nb), slice(i * lb // count, (i + 1) * lb // count)) for i in range(count)]


def _flat(v):
    return v.reshape(v.shape[0] * v.shape[1], v.shape[2])


def _inproj_kernel(x_ref, sh_ref, sc_ref, g_ref, w_ref, *out_refs, splits):
    nb, lb, d = x_ref.shape
    parts = _row_parts(nb, lb, ROW_PARTS)
    hbs = [_flat(_rms(x_ref[bs, rs, :], g_ref[...]) * (1.0 + sc_ref[bs]) + sh_ref[bs]).astype(BF16)
           for bs, rs in parts]
    off = 0
    for o_ref, n in zip(out_refs, splits):
        for (bs, rs), hb in zip(parts, hbs):
            res = jnp.dot(hb, w_ref[:, off:off + n], preferred_element_type=F32)
            o_ref[bs, rs, :] = res.reshape(bs.stop - bs.start, rs.stop - rs.start, n)
        off += n


def _inproj_gdn_kernel(x_ref, sh_ref, sc_ref, g_ref, w_ref, cw_ref, cb_ref, cbuf_ref,
                       qkv_ref, gt_ref, ba_ref, cbnew_ref, xbuf):
    nb, lb, d = x_ref.shape
    hd = HEAD_DIM
    qk = HEADS * hd
    w3 = 3 * qk
    t = pl.program_id(1)
    parts = _row_parts(nb, lb, ROW_PARTS)
    shape = lambda bs, rs: (bs.stop - bs.start, rs.stop - rs.start)

    @pl.when(t == 0)
    def _():
        xbuf[:, PAD_ROWS - (GDN_CONV - 1):PAD_ROWS, :] = cbuf_ref[...]

    @pl.when(t > 0)
    def _():
        xbuf[:, 0:PAD_ROWS, :] = xbuf[:, lb:lb + PAD_ROWS, :]

    hbs = [_flat(_rms(x_ref[bs, rs, :], g_ref[...]) * (1.0 + sc_ref[bs]) + sh_ref[bs]).astype(BF16)
           for bs, rs in parts]
    raws = [jnp.dot(hb, w_ref[:, 0:w3], preferred_element_type=F32) for hb in hbs]
    for (bs, rs), raw in zip(parts, raws):
        xbuf[bs, PAD_ROWS + rs.start:PAD_ROWS + rs.stop, :] = raw.reshape(shape(bs, rs) + (w3,))
    cbnew_ref[...] = xbuf[:, lb + PAD_ROWS - (GDN_CONV - 1):lb + PAD_ROWS, :]

    cw = cw_ref[...]
    for (bs, rs), hb, raw in zip(parts, hbs, raws):
        pb, pr = shape(bs, rs)
        y = cb_ref[...] + raw.reshape(pb, pr, w3) * cw[GDN_CONV - 1:GDN_CONV]
        for j in range(GDN_CONV - 1):
            s = PAD_ROWS - (GDN_CONV - 1) + j
            y = y + xbuf[bs, s + rs.start:s + rs.stop, :] * cw[j:j + 1]
        a = _silu(y)
        for h in range(2 * HEADS):
            hs = slice(h * hd, (h + 1) * hd)
            v = a[:, :, hs]
            scale = lax.rsqrt(jnp.sum(v * v, axis=-1, keepdims=True) + EPS)
            qkv_ref[bs, rs, hs] = v * (scale * (hd ** -0.5) if h < HEADS else scale)
        qkv_ref[bs, rs, 2 * qk:w3] = a[:, :, 2 * qk:w3]
        gt_ref[bs, rs, :] = jnp.dot(hb, w_ref[:, w3:w3 + qk], preferred_element_type=F32).reshape(pb, pr, qk)
        ba_ref[bs, rs, :] = jnp.dot(hb, w_ref[:, w3 + qk:], preferred_element_type=F32).reshape(pb, pr, hd)


def _inproj_gdn_call(x, mod, gain, w, j, cw, cb, cbuf, nb, lb):
    bsz, seq, d = x.shape
    ntot = w.shape[2]
    qk = HEADS * HEAD_DIM
    w3 = 3 * qk
    row = lambda k: pl.BlockSpec((nb, 1, d), lambda b, t, k=k: (b, 0, k))
    tok = lambda n: pl.BlockSpec((nb, lb, n), lambda b, t: (b, t, 0))
    taps = GDN_CONV - 1
    return pl.pallas_call(
        _inproj_gdn_kernel,
        grid=(bsz // nb, seq // lb),
        in_specs=[tok(d), row(0), row(1),
                  pl.BlockSpec((1, d), lambda b, t: (0, 0)),
                  _layer_weight(d, ntot, j),
                  pl.BlockSpec((GDN_CONV, w3), lambda b, t: (0, 0)),
                  pl.BlockSpec((1, w3), lambda b, t: (0, 0)),
                  pl.BlockSpec((None, nb, taps, w3), lambda b, t: (j, b, 0, 0))],
        out_specs=[tok(w3), tok(qk), tok(HEAD_DIM), pl.BlockSpec((nb, taps, w3), lambda b, t: (b, 0, 0))],
        out_shape=[jax.ShapeDtypeStruct((bsz, seq, w3), F32), jax.ShapeDtypeStruct((bsz, seq, qk), F32),
                   jax.ShapeDtypeStruct((bsz, seq, HEAD_DIM), F32),
                   jax.ShapeDtypeStruct((bsz, taps, w3), F32)],
        scratch_shapes=[pltpu.VMEM((nb, lb + PAD_ROWS, w3), F32)],
        compiler_params=_params("arbitrary", "arbitrary"),
        name="inproj_gdn",
    )(x, mod, mod, gain.reshape(1, d), w, cw, cb.reshape(1, w3), cbuf)


def _inproj_call(x, mod, gain, w, j, splits, nb, lb):
    bsz, seq, d = x.shape
    ntot = w.shape[2]
    row = lambda k: pl.BlockSpec((nb, 1, d), lambda b, t, k=k: (b, 0, k))
    return pl.pallas_call(
        functools.partial(_inproj_kernel, splits=splits),
        grid=(bsz // nb, seq // lb),
        in_specs=[pl.BlockSpec((nb, lb, d), lambda b, t: (b, t, 0)),
                  row(0), row(1),
                  pl.BlockSpec((1, d), lambda b, t: (0, 0)),
                  _layer_weight(d, ntot, j)],
        out_specs=[pl.BlockSpec((nb, lb, n), lambda b, t: (b, t, 0)) for n in splits],
        out_shape=[jax.ShapeDtypeStruct((bsz, seq, n), F32) for n in splits],
        compiler_params=_params("arbitrary", "arbitrary"),
        name="inproj",
    )(x, mod, mod, gain.reshape(1, d), w)


def _ffn_kernel(og_ref, x_ref, g1_ref, sh2_ref, sc2_ref, g2_ref, npost_ref, npre_ref, npostf_ref,
                wout_ref, wg_ref, wu_ref, cw_ref, cb_ref, wd_ref, fbuf_ref,
                xo_ref, fbnew_ref, gbuf):
    nb, lb, d = x_ref.shape
    f = wg_ref.shape[1]
    t = pl.program_id(1)
    parts = _row_parts(nb, lb, ROW_PARTS)
    shape = lambda bs, rs: (bs.stop - bs.start, rs.stop - rs.start)
    flat = _flat

    @pl.when(t == 0)
    def _():
        gbuf[:, PAD_ROWS - (FFN_CONV - 1):PAD_ROWS, :] = fbuf_ref[...]

    @pl.when(t > 0)
    def _():
        gbuf[:, 0:PAD_ROWS, :] = gbuf[:, lb:lb + PAD_ROWS, :]

    mix = [jnp.dot(flat(og_ref[bs, rs, :]).astype(BF16), wout_ref[...], preferred_element_type=F32)
           for bs, rs in parts]
    x1, hb = [], []
    for (bs, rs), mx in zip(parts, mix):
        pb, pr = shape(bs, rs)
        xa = x_ref[bs, rs, :] + (1.0 + g1_ref[bs]) * _rms(mx.reshape(pb, pr, d), npost_ref[...])
        x1.append(xa)
        hb.append(flat(_rms(xa, npre_ref[...]) * (1.0 + sc2_ref[bs]) + sh2_ref[bs]).astype(BF16))
    gt = [jnp.dot(h, wg_ref[...], preferred_element_type=F32) for h in hb]
    up = [jnp.dot(h, wu_ref[...], preferred_element_type=F32) for h in hb]
    for (bs, rs), g in zip(parts, gt):
        pb, pr = shape(bs, rs)
        gbuf[bs, PAD_ROWS + rs.start:PAD_ROWS + rs.stop, :] = g.reshape(pb, pr, f)
    fbnew_ref[...] = gbuf[:, lb + PAD_ROWS - (FFN_CONV - 1):lb + PAD_ROWS, :]

    cw = cw_ref[...]
    out = []
    for (bs, rs), g, u in zip(parts, gt, up):
        pb, pr = shape(bs, rs)
        y = cb_ref[...] + g.reshape(pb, pr, f) * cw[FFN_CONV - 1:FFN_CONV]
        for j in range(FFN_CONV - 1):
            s = PAD_ROWS - (FFN_CONV - 1) + j
            y = y + gbuf[bs, s + rs.start:s + rs.stop, :] * cw[j:j + 1]
        act = flat(_silu(y)) * u
        out.append(jnp.dot(act.astype(BF16), wd_ref[...], preferred_element_type=F32))
    for (bs, rs), xa, o in zip(parts, x1, out):
        pb, pr = shape(bs, rs)
        xo_ref[bs, rs, :] = xa + (1.0 + g2_ref[bs]) * _rms(o.reshape(pb, pr, d), npostf_ref[...])


def _layer_weight(rows, cols, layer, col_block=0):
    return pl.BlockSpec((None, rows, cols), lambda b, t: (layer, 0, col_block), pipeline_mode=pl.Buffered(1))


def _ffn_call(og, x, mod, npost, npre, npostf, wout, jw, wgu, cw, cb, wd, layer, fbuf, nb, lb):
    bsz, seq, d = x.shape
    f = wd.shape[1]
    row = lambda k: pl.BlockSpec((nb, 1, d), lambda b, t, k=k: (b, 0, k))
    vec = lambda n: pl.BlockSpec((1, n), lambda b, t: (0, 0))
    tok = pl.BlockSpec((nb, lb, d), lambda b, t: (b, t, 0))
    taps = FFN_CONV - 1
    return pl.pallas_call(
        _ffn_kernel,
        grid=(bsz // nb, seq // lb),
        in_specs=[tok, tok, row(2), row(3), row(4), row(5), vec(d), vec(d), vec(d),
                  _layer_weight(d, d, jw), _layer_weight(d, f, layer, 0), _layer_weight(d, f, layer, 1),
                  pl.BlockSpec((FFN_CONV, f), lambda b, t: (0, 0)), vec(f), _layer_weight(f, d, layer),
                  pl.BlockSpec((None, nb, taps, f), lambda b, t: (layer, b, 0, 0))],
        out_specs=[tok, pl.BlockSpec((nb, taps, f), lambda b, t: (b, 0, 0))],
        out_shape=[jax.ShapeDtypeStruct((bsz, seq, d), F32),
                   jax.ShapeDtypeStruct((bsz, taps, f), F32)],
        scratch_shapes=[pltpu.VMEM((nb, lb + PAD_ROWS, f), F32)],
        compiler_params=_params("arbitrary", "arbitrary"),
        name="ffn_block",
    )(og, x, mod, mod, mod, mod, npost.reshape(1, d), npre.reshape(1, d), npostf.reshape(1, d),
      wout, wgu, wgu, cw, cb.reshape(1, f), wd, fbuf)


def _seq_masks(nsq, lc):
    c = nsq * lc
    i = np.arange(c)[:, None]
    j = np.arange(c)[None, :]
    same = (i // lc) == (j // lc)
    return i, j, same


def _gdn_consts(nsq, lc):
    i, j, same = _seq_masks(nsq, lc)
    return jnp.asarray((same & (j <= i)).astype(np.float32), BF16)


def _hgrn_levels(lc):
    s, out = lc, []
    while s >= 4:
        out.append(s)
        s //= 2
    return out


def _hgrn_consts(nsq, lc):
    i, j, same = _seq_masks(nsq, lc)
    mats = [same & (j <= i), same & (j > i)]
    for s in _hgrn_levels(lc):
        mid = (i // s) * s + s // 2
        mats.append(np.where(i >= mid, (j >= mid) & (j <= i), (j > i) & (j < mid)))
    return jnp.asarray(np.concatenate(mats, axis=0).astype(np.float32), BF16)


def _state_index(ch, n, nsq, lc, lr):
    return ((ch * nsq + n) * lc) // lr


def _store_heads(og_ref, ch, nsq, lc, val):
    for h in range(HEADS):
        hs = slice(h * HEAD_DIM, (h + 1) * HEAD_DIM)
        if og_ref.shape[0] == 1:
            og_ref[0, ch * CHUNK:(ch + 1) * CHUNK, hs] = val[h]
        else:
            og_ref[ch * nsq:(ch + 1) * nsq, :, hs] = val[h].reshape(nsq, lc, HEAD_DIM)


def _gdn_kernel(qkv_ref, gt_ref, ba_ref, s0_ref, alog_ref, dtb_ref, ng_ref, tri_ref, *rest, nch, nsq, lc):
    og_ref, snew_ref, s_scr = rest[-3:]
    ns, lr, w3 = qkv_ref.shape
    r = ns * lr
    hd = HEAD_DIM
    qk = HEADS * hd
    t = pl.program_id(1)
    last = pl.num_programs(1) - 1

    @pl.when(t == 0)
    def _():
        s_scr[...] = s0_ref[...]

    qkv = qkv_ref[...].reshape(r, w3)
    gt = gt_ref[...].reshape(r, qk)

    ba = ba_ref[...].reshape(r, hd)
    beta = _sigmoid(ba)
    g = -jnp.exp(alog_ref[...]) * _softplus(ba + dtb_ref[...])
    chunks = [slice(ch * CHUNK, (ch + 1) * CHUNK) for ch in range(nch)]
    gcum = [_dot_sel(tri_ref[...], g[sl], 3) for sl in chunks]
    per_item = lambda f: jnp.concatenate([f(ch, h)[None] for ch in range(nch) for h in range(HEADS)], axis=0)
    gc = per_item(lambda ch, h: gcum[ch][:, HEADS + h:HEADS + h + 1])
    bcol = per_item(lambda ch, h: beta[chunks[ch], h:h + 1])

    q = _items(qkv, nch, 0, hd)
    k = _items(qkv, nch, qk, hd)
    v = _items(qkv, nch, 2 * qk, hd)
    eg = jnp.exp(gc)
    kb = k * bcol
    qg = q * eg
    rhs = jnp.concatenate([v * bcol, kb * eg], axis=-1).astype(BF16)
    gate = _silu(_items(gt, nch, 0, hd))
    rid = _idiv(lax.broadcasted_iota(jnp.int32, (1, CHUNK, 1), 1), lc)

    lane = lax.broadcasted_iota(jnp.int32, (1, 1, 2 * CHUNK), 2)
    col = lane & (CHUNK - 1)
    row = lax.broadcasted_iota(jnp.int32, (1, CHUNK, 1), 1)
    same = _idiv(row, lc) == _idiv(col, lc)
    incl = same & (col <= row)
    strict = same & (col < row)
    gcp = jnp.where(lane < CHUNK, _even(gc), _odd(gc))
    grp = jnp.sum(jnp.where(col == row, gcp, 0.0), axis=1, keepdims=True)
    decay = jnp.where(incl, jnp.exp(jnp.where(incl, gcp - grp, 0.0)), 0.0)
    kst = _stack(k.astype(BF16))
    kk_qk = _bmm_nt(jnp.concatenate([_side(kb.astype(BF16)), _side(q.astype(BF16))], axis=1), kst)
    a = jnp.where(strict, kk_qk[:, :CHUNK] * decay, 0.0)
    attn = jnp.where(incl, kk_qk[:, CHUNK:] * decay, 0.0)
    tinv = _packed_unit_lower_inverse(a, min(16, lc), lc, _idiv(lane, CHUNK), row, col)
    th, tl = _split2(tinv)
    sol = _bmm(jnp.concatenate([th, tl], axis=1), _stack(rhs))
    sol = sol[:, :CHUNK] + sol[:, CHUNK:]
    uv, w = _halves(sol, 0, hd), _halves(sol, hd, hd)

    for ch in range(nch):
        it = slice(ch * HEADS, (ch + 1) * HEADS)
        pr = slice(ch * HEADS // 2, (ch + 1) * HEADS // 2)
        us, os_ = [], []
        for n in range(nsq):
            sl = slice(n * lc, (n + 1) * lc)
            st = s_scr[_state_index(ch, n, nsq, lc, lr)]
            ws = _bmm(jnp.concatenate([w[it, sl], qg[it, sl]], axis=1), st)
            us.append(uv[it, sl] - ws[:, :lc])
            os_.append(ws[:, lc:])
        u = us[0] if nsq == 1 else jnp.concatenate(us, axis=1)
        o = (os_[0] if nsq == 1 else jnp.concatenate(os_, axis=1)) + _halves(_bmm(attn[pr], _stack(u.astype(BF16))), 0, hd)
        for n in range(nsq):
            si = _state_index(ch, n, nsq, lc, lr)
            gl = gc[it, (n + 1) * lc - 1:(n + 1) * lc, :]
            if nsq > 1:
                kd = jnp.where(rid == n, k[it] * jnp.exp(jnp.where(rid == n, gl - gc[it], 0.0)), 0.0)
            else:
                kd = k[it] * jnp.exp(gl - gc[it])
            s_scr[si] = s_scr[si] * jnp.exp(gl) + _bmm(jnp.swapaxes(kd, 1, 2), u)
        _store_heads(og_ref, ch, nsq, lc, _rms(o, ng_ref[...]) * gate[it])

    @pl.when(t == last)
    def _():
        for n in range(ns):
            _write_state(snew_ref, n, s_scr[n])


def _stacked_state(states, j, prev, ns):
    tail = states.shape[2:]
    in_spec = pl.BlockSpec((None, ns) + tail, lambda b, t: (j, b, 0, 0, 0))
    if prev is None:
        out_spec = pl.BlockSpec((states.shape[0], ns) + tail, lambda b, t: (0, b, 0, 0, 0))
        extra_in, extra_specs = [], []
    else:
        out_spec = in_spec
        extra_in, extra_specs = [prev], [pl.BlockSpec(memory_space=pl.ANY)]
    return in_spec, out_spec, extra_in, extra_specs, jax.ShapeDtypeStruct(states.shape, states.dtype)


def _write_state(snew_ref, n, value):
    if len(snew_ref.shape) == 5:
        for m in range(snew_ref.shape[0]):
            snew_ref[m, n] = value
    else:
        snew_ref[n] = value


def _gdn_call(qkv, gt, ba, states, j, prev, alog, dtb, ng, nch, nsq, lc):
    assert 2 * CHUNK == LANES and HEADS % 2 == 0, "two heads' (CHUNK, CHUNK) matrices fill one lane tile"
    bsz, seq, w3 = qkv.shape
    qk = w3 // 3
    hd = HEAD_DIM
    tri = _gdn_consts(nsq, lc)
    ns, lr = (1, nch * CHUNK) if nsq == 1 else (nch * nsq, lc)
    tok = lambda n: pl.BlockSpec((ns, lr, n), lambda b, t: (b, t, 0))
    vec = lambda n: pl.BlockSpec((1, n), lambda b, t: (0, 0))
    state, state_out, extra_in, extra_specs, state_shape = _stacked_state(states, j, prev, ns)
    inputs = [qkv, gt, ba, states, alog, dtb, ng.reshape(1, hd), tri] + extra_in
    return pl.pallas_call(
        functools.partial(_gdn_kernel, nch=nch, nsq=nsq, lc=lc),
        grid=(bsz // ns, seq // lr),
        in_specs=[tok(w3), tok(qk), tok(hd), state, vec(hd), vec(hd), vec(hd),
                  pl.BlockSpec((CHUNK, CHUNK), lambda b, t: (0, 0))] + extra_specs,
        out_specs=[tok(qk), state_out],
        out_shape=[jax.ShapeDtypeStruct((bsz, seq, qk), F32), state_shape],
        input_output_aliases={len(inputs) - 1: 1} if extra_in else {},
        scratch_shapes=[pltpu.VMEM((ns, HEADS, hd, hd), F32)],
        compiler_params=_params("arbitrary", "arbitrary"),
        name="gdn_core",
    )(*inputs)


def _hgrn_kernel(proj_ref, s0_ref, lbraw_ref, ng_ref, cm_ref, *rest, nch, nsq, lc, layer):
    og_ref, snew_ref, st_scr = rest[-3:]
    ns, lr, w4 = proj_ref.shape
    r = ns * lr
    hd = HEAD_DIM
    qf = HEADS * hd
    t = pl.program_id(1)
    last = pl.num_programs(1) - 1

    @pl.when(t == 0)
    def _():
        for n in range(ns):
            st_scr[n] = jnp.swapaxes(s0_ref[n], 1, 2)

    lbraw = lbraw_ref[...]
    e = jnp.exp(lbraw - jnp.max(lbraw, axis=0, keepdims=True))
    p = e / jnp.sum(e, axis=0, keepdims=True)
    lb = jnp.zeros((1, qf), F32)
    for m in range(1, layer + 1):
        lb = lb + p[m:m + 1, :]

    proj = proj_ref[...].reshape(r, w4)
    fr = proj[:, qf:2 * qf]
    efr = jnp.exp(-jnp.abs(fr))
    big = 1.0 / (1.0 + efr)
    small = efr * big
    sig_pos = jnp.where(fr >= 0.0, big, small)
    sig_neg = jnp.where(fr >= 0.0, small, big)
    forget = jnp.maximum(lb, LB_FLOOR) + (1.0 - lb) * sig_pos
    logf = jnp.log(forget)

    levels = _hgrn_levels(lc)
    nmat = 2 + len(levels)
    cm = cm_ref[...]
    ex = jnp.concatenate([jnp.exp(_dot_sel(cm, logf[ch * CHUNK:(ch + 1) * CHUNK], 2)) for ch in range(nch)], axis=0)
    factor = lambda m: jnp.concatenate(
        [ex[(ch * nmat + m) * CHUNK:(ch * nmat + m + 1) * CHUNK, h * hd:(h + 1) * hd][None]
         for ch in range(nch) for h in range(HEADS)], axis=0)

    q = _silu(_items(proj, nch, 0, hd))
    k = _items((1.0 - lb) * sig_neg, nch, 0, hd)
    v = _items(proj, nch, 2 * qf, hd)
    gate = _silu(_items(proj, nch, 3 * qf, hd))

    row = lax.broadcasted_iota(jnp.int32, (CHUNK, CHUNK), 0)
    col = lax.broadcasted_iota(jnp.int32, (CHUNK, CHUNK), 1)
    qb, kb = q.astype(BF16), k.astype(BF16)
    attn = jnp.where((row == col)[None], _bmm_nt(qb, kb), 0.0)
    odd = (lax.broadcasted_iota(jnp.int32, (1, CHUNK, 1), 1) & 1) == 1
    factors = [jnp.where(odd, _items(forget, nch, 0, hd), 1.0)] + [factor(2 + li) for li in range(len(levels))]
    for s, fl in zip([2] + levels, factors):
        mask = (_idiv(row, s) == _idiv(col, s)) & ((row & (s - 1)) >= s // 2) & ((col & (s - 1)) < s // 2)
        flb = fl.astype(BF16)
        attn = jnp.where(mask[None], _bmm_nt(qb * flb, kb * flb), attn)
    eg = factor(0)
    qe = q * eg
    kd = k * factor(1)
    cid = _idiv(lax.broadcasted_iota(jnp.int32, (1, 1, CHUNK), 2), lc)

    for ch in range(nch):
        it = slice(ch * HEADS, (ch + 1) * HEADS)
        vt = jnp.swapaxes(v[it], 1, 2)
        os_ = []
        for n in range(nsq):
            sl = slice(n * lc, (n + 1) * lc)
            si = _state_index(ch, n, nsq, lc, lr)
            st = st_scr[si]
            os_.append(_bmm_nt(qe[it, sl], st))
            vtn = vt if nsq == 1 else jnp.where(cid == n, vt, 0.0)
            egl = eg[it, (n + 1) * lc - 1:(n + 1) * lc, :]
            st_scr[si] = st * egl + _bmm(vtn, kd[it])
        o = (os_[0] if nsq == 1 else jnp.concatenate(os_, axis=1)) + _bmm(attn[it], v[it])
        _store_heads(og_ref, ch, nsq, lc, _rms(o, ng_ref[...]) * gate[it])

    @pl.when(t == last)
    def _():
        for n in range(ns):
            _write_state(snew_ref, n, jnp.swapaxes(st_scr[n], 1, 2))


def _hgrn_call(proj, states, j, prev, lbraw, ng, nch, nsq, lc):
    bsz, seq, w4 = proj.shape
    qf = w4 // 4
    hd = HEAD_DIM
    cm = _hgrn_consts(nsq, lc)
    ns, lr = (1, nch * CHUNK) if nsq == 1 else (nch * nsq, lc)
    tok = lambda n: pl.BlockSpec((ns, lr, n), lambda b, t: (b, t, 0))
    state, state_out, extra_in, extra_specs, state_shape = _stacked_state(states, j, prev, ns)
    inputs = [proj, states, lbraw, ng.reshape(1, hd), cm] + extra_in
    return pl.pallas_call(
        functools.partial(_hgrn_kernel, nch=nch, nsq=nsq, lc=lc, layer=j),
        grid=(bsz // ns, seq // lr),
        in_specs=[tok(w4), state,
                  pl.BlockSpec(lbraw.shape, lambda b, t: (0, 0)),
                  pl.BlockSpec((1, hd), lambda b, t: (0, 0)),
                  pl.BlockSpec(cm.shape, lambda b, t: (0, 0))] + extra_specs,
        out_specs=[tok(qf), state_out],
        out_shape=[jax.ShapeDtypeStruct((bsz, seq, qf), F32), state_shape],
        input_output_aliases={len(inputs) - 1: 1} if extra_in else {},
        scratch_shapes=[pltpu.VMEM((ns, HEADS, hd, hd), F32)],
        compiler_params=_params("arbitrary", "arbitrary"),
        name="hgrn_core",
    )(*inputs)


def _prep_weights(gdn_w_in, gdn_a_log, gdn_dt_bias, hgrn_w_in, gdn_w_out, hgrn_w_out, ffn_w_gu, ffn_w_down):
    pad = HEAD_DIM - 2 * HEADS
    w_gdn = jnp.pad(gdn_w_in, ((0, 0), (0, 0), (0, pad))).astype(BF16)
    gate_pad = lambda a: jnp.pad(a, ((0, 0), (HEADS, HEAD_DIM - 2 * HEADS)))[:, None, :]
    return dict(
        w_gdn=w_gdn, alog=gate_pad(gdn_a_log), dtb=gate_pad(gdn_dt_bias),
        w_hgrn=hgrn_w_in.astype(BF16), gdn_w_out=gdn_w_out.astype(BF16), hgrn_w_out=hgrn_w_out.astype(BF16),
        wgu=ffn_w_gu.astype(BF16), wd=ffn_w_down.astype(BF16))


def _trunk(x, mod, s_gdn, s_gconv, s_hgrn, s_fconv, wts, p, nb, lb, nsq, lc):
    depth = mod.shape[0]
    bsz, seq, _ = x.shape
    qk = HEADS * HEAD_DIM
    if nsq == 1:
        chunks_avail = seq // CHUNK
    else:
        state_bytes = (3 + 2 * max(s_gdn.shape[0], s_hgrn.shape[0])) * nsq * HEADS * HEAD_DIM * HEAD_DIM * 4
        chunks_avail = max(1, min(bsz // nsq, STATE_VMEM_BUDGET // state_bytes))
    gdn_nch = min(GDN_CHUNKS, chunks_avail)
    hgrn_nch = min(HGRN_CHUNKS, chunks_avail)
    new_gdn, new_gconv, new_hgrn, new_fconv = None, [], None, []
    for layer in range(depth):
        modl = mod[layer][:, None, :]
        j = layer // N_MIXERS
        if layer % N_MIXERS == 0:
            qkv, gt, ba, cb_new = _inproj_gdn_call(x, modl, wts["norm_pre_mix"][layer], p["w_gdn"], j,
                                                   wts["gdn_conv_w"][j], wts["gdn_conv_b"][j], s_gconv, nb, lb)
            og, new_gdn = _gdn_call(qkv, gt, ba, s_gdn, j, new_gdn, p["alog"][j], p["dtb"][j],
                                    wts["gdn_norm"][j], gdn_nch, nsq, lc)
            new_gconv.append(cb_new)
            w_out = p["gdn_w_out"]
        else:
            (proj,) = _inproj_call(x, modl, wts["norm_pre_mix"][layer], p["w_hgrn"], j, (4 * qk,), nb, lb)
            og, new_hgrn = _hgrn_call(proj, s_hgrn, j, new_hgrn, wts["hgrn_lb"], wts["hgrn_norm"][j],
                                      hgrn_nch, nsq, lc)
            w_out = p["hgrn_w_out"]
        x, fb_new = _ffn_call(og, x, modl, wts["norm_post_mix"][layer], wts["norm_pre_ffn"][layer],
                              wts["norm_post_ffn"][layer], w_out, j, p["wgu"],
                              wts["ffn_conv_w"][layer], wts["ffn_conv_b"][layer], p["wd"], layer, s_fconv, nb, lb)
        new_fconv.append(fb_new)
    return x, new_gdn, jnp.stack(new_gconv), new_hgrn, jnp.stack(new_fconv)


def kernel(x_prompt, x_sample, state_gdn, state_gdn_conv, state_hgrn, state_ffn_conv, c_prompt, c_sample, ada_w, ada_b, norm_pre_mix, norm_post_mix, norm_pre_ffn, norm_post_ffn, gdn_w_in, gdn_conv_w, gdn_conv_b, gdn_a_log, gdn_dt_bias, gdn_norm, gdn_w_out, hgrn_lb, hgrn_w_in, hgrn_norm, hgrn_w_out, ffn_w_gu, ffn_conv_w, ffn_conv_b, ffn_w_down):
    wts = dict(norm_pre_mix=norm_pre_mix, norm_post_mix=norm_post_mix, norm_pre_ffn=norm_pre_ffn,
               norm_post_ffn=norm_post_ffn, gdn_conv_w=gdn_conv_w, gdn_conv_b=gdn_conv_b, gdn_norm=gdn_norm,
               hgrn_lb=hgrn_lb, hgrn_norm=hgrn_norm, ffn_conv_w=ffn_conv_w, ffn_conv_b=ffn_conv_b)
    p = _prep_weights(gdn_w_in, gdn_a_log, gdn_dt_bias, hgrn_w_in, gdn_w_out, hgrn_w_out, ffn_w_gu, ffn_w_down)

    bp, sp, _ = x_prompt.shape
    bs, ss, _ = x_sample.shape
    mod = _mod_call(jnp.concatenate([c_prompt, c_sample], axis=0), ada_w, ada_b)
    mod_p, mod_s = mod[:, :bp], mod[:, bp:]

    dt = x_prompt.dtype
    zeros = lambda a: jnp.zeros((a.shape[0], bp) + a.shape[2:], dt)
    y_p, p_gdn, p_gconv, p_hgrn, p_fconv = _trunk(
        x_prompt, mod_p, zeros(state_gdn), zeros(state_gdn_conv), zeros(state_hgrn), zeros(state_ffn_conv),
        wts, p, nb=1, lb=min(TILE_ROWS, sp), nsq=1, lc=min(CHUNK, sp))
    y_s, s_gdn, s_gconv, s_hgrn, s_fconv = _trunk(
        x_sample, mod_s, state_gdn, state_gdn_conv, state_hgrn, state_ffn_conv,
        wts, p, nb=min(bs, SHORT_TILE_ROWS // ss), lb=ss, nsq=min(bs, CHUNK // ss), lc=ss)
    return (y_p, y_s, p_gdn, p_gconv, p_hgrn, p_fconv, s_gdn, s_gconv, s_hgrn, s_fconv)
```

```python
import functools

import numpy as np
import jax
import jax.numpy as jnp
from jax import lax
from jax.experimental import pallas as pl
from jax.experimental.pallas import tpu as pltpu

F32 = jnp.float32
BF16 = jnp.bfloat16
EPS = 1e-6
LB_FLOOR = 1e-30
N_MIXERS = 2
LANES = 128
HEADS = 8
HEAD_DIM = 128
GDN_CONV = 4
FFN_CONV = 3
CHUNK = 64
PAD_ROWS = 8
TILE_ROWS = 512
SHORT_TILE_ROWS = 256
GDN_CHUNKS = 8
HGRN_CHUNKS = 8
ROW_PARTS = 2
VMEM_LIMIT = 56 * 1024 * 1024
STATE_VMEM_BUDGET = 40 * 1024 * 1024


def _params(*sem):
    return pltpu.CompilerParams(dimension_semantics=sem, vmem_limit_bytes=VMEM_LIMIT)


def _sigmoid(x):
    return jax.nn.sigmoid(x)


def _silu(x):
    hx = 0.5 * x
    return hx + hx * jnp.tanh(hx)


def _softplus(x):
    return jnp.maximum(x, 0.0) + jnp.log1p(jnp.exp(-jnp.abs(x)))


def _rms(x, g):
    return x * lax.rsqrt(jnp.mean(x * x, axis=-1, keepdims=True) + EPS) * g


def _dot(a, b):
    return jnp.dot(a.astype(BF16), b.astype(BF16), preferred_element_type=F32)


def _bmm(a, b):
    return lax.dot_general(a.astype(BF16), b.astype(BF16), (((2,), (1,)), ((0,), (0,))),
                           preferred_element_type=F32)


def _bmm_nt(a, b):
    return lax.dot_general(a.astype(BF16), b.astype(BF16), (((2,), (2,)), ((0,), (0,))),
                           preferred_element_type=F32)


def _split2(x):
    hi = x.astype(BF16)
    lo = (x - hi.astype(F32)).astype(BF16)
    return hi, lo


def _even(x):
    return jnp.concatenate([x[i:i + 1] for i in range(0, x.shape[0], 2)], axis=0)


def _odd(x):
    return jnp.concatenate([x[i:i + 1] for i in range(1, x.shape[0], 2)], axis=0)


def _side(x):
    return jnp.concatenate([_even(x), _odd(x)], axis=-1)


def _stack(x):
    e, o = _even(x), _odd(x)
    return jnp.concatenate([jnp.concatenate([e, jnp.zeros_like(e)], axis=-1),
                            jnp.concatenate([jnp.zeros_like(o), o], axis=-1)], axis=1)


def _halves(x, off, width):
    half = x.shape[-1] // 2
    return jnp.concatenate([x[p:p + 1, :, s * half + off:s * half + off + width]
                            for p in range(x.shape[0]) for s in range(2)], axis=0)


def _pair_product(lhs, rhs, grp):
    c = rhs.shape[1]
    n = len(lhs)
    bd = lambda x: jnp.concatenate([jnp.where(grp == m, x, jnp.zeros_like(x))
                                    for m in range(x.shape[-1] // c)], axis=1)
    rh, rl = _split2(rhs)
    pieces = [_split2(x) for x in lhs]
    his = [hi for hi, _ in pieces]
    los = [lo for _, lo in pieces]
    top = _bmm(jnp.concatenate(his + los, axis=1), bd(rh))
    bot = _bmm(his[0] if n == 1 else jnp.concatenate(his, axis=1), bd(rl))
    return [top[:, i * c:(i + 1) * c] + top[:, (n + i) * c:(n + i + 1) * c] + bot[:, i * c:(i + 1) * c]
            for i in range(n)]


def _dot_sel(sel, x, pieces):
    acc = None
    r = x
    for i in range(pieces):
        part = r.astype(BF16)
        if i + 1 < pieces:
            r = r - part.astype(F32)
        term = jnp.dot(sel, part, preferred_element_type=F32)
        acc = term if acc is None else acc + term
    return acc


def _idiv(x, p2):
    return x >> (p2.bit_length() - 1)


def _packed_unit_lower_inverse(a, block, span, grp, row, col):
    eye = (col == row).astype(F32)
    if block >= span:
        return _packed_neumann(eye, a, span, grp)
    diag = _idiv(row, block) == _idiv(col, block)
    inv = _packed_neumann(eye, jnp.where(diag, a, 0.0), block, grp)
    (low,) = _pair_product([inv], jnp.where(diag, 0.0, a), grp)
    blk = _packed_neumann(eye, low, span // block, grp)
    (res,) = _pair_product([blk], inv, grp)
    return res


def _packed_neumann(eye, d, order, grp):
    p = eye - d
    if order <= 2:
        return p
    (dp,) = _pair_product([d], d, grp)
    k = 4
    while k < order:
        pp, dp_next = _pair_product([p, dp], dp, grp)
        p, dp = p + pp, dp_next
        k *= 2
    (pp,) = _pair_product([p], dp, grp)
    return p + pp


def _items(x, nch, base, width):
    return jnp.concatenate(
        [x[ch * CHUNK:(ch + 1) * CHUNK, base + h * width:base + (h + 1) * width][None]
         for ch in range(nch) for h in range(HEADS)], axis=0)


def _mod_kernel(c_ref, w_ref, b_ref, o_ref):
    cs = _silu(c_ref[...])
    o_ref[0] = _dot(cs, w_ref[0]) + b_ref[0]


def _mod_call(c_all, ada_w, ada_b):
    depth, d, n = ada_w.shape
    rows = c_all.shape[0]
    tn = n // 4
    return pl.pallas_call(
        _mod_kernel,
        grid=(depth, n // tn),
        in_specs=[pl.BlockSpec((rows, d), lambda l, j: (0, 0)),
                  pl.BlockSpec((1, d, tn), lambda l, j: (l, 0, j)),
                  pl.BlockSpec((1, 1, tn), lambda l, j: (l, 0, j))],
        out_specs=pl.BlockSpec((1, rows, tn), lambda l, j: (l, 0, j)),
        out_shape=jax.ShapeDtypeStruct((depth, rows, n), F32),
        compiler_params=_params("arbitrary", "arbitrary"),
        name="adaln_mod",
    )(c_all, ada_w, ada_b.reshape(depth, 1, n))


def _row_parts(nb, lb, count):
    if nb >= count:
        return [(slice(i * nb // count, (i + 1) * nb // count), slice(0, lb)) for i in range(count)]
    return [(slice(0, nb), slice(i * lb // count, (i + 1) * lb // count)) for i in range(count)]


def _flat(v):
    return v.reshape(v.shape[0] * v.shape[1], v.shape[2])


def _inproj_kernel(x_ref, sh_ref, sc_ref, g_ref, w_ref, *out_refs, splits):
    nb, lb, d = x_ref.shape
    parts = _row_parts(nb, lb, ROW_PARTS)
    hbs = [_flat(_rms(x_ref[bs, rs, :], g_ref[...]) * (1.0 + sc_ref[bs]) + sh_ref[bs]).astype(BF16)
           for bs, rs in parts]
    off = 0
    for o_ref, n in zip(out_refs, splits):
        for (bs, rs), hb in zip(parts, hbs):
            res = jnp.dot(hb, w_ref[:, off:off + n], preferred_element_type=F32)
            o_ref[bs, rs, :] = res.reshape(bs.stop - bs.start, rs.stop - rs.start, n)
        off += n


def _inproj_gdn_kernel(x_ref, sh_ref, sc_ref, g_ref, w_ref, cw_ref, cb_ref, cbuf_ref,
                       qkv_ref, gt_ref, ba_ref, cbnew_ref, xbuf):
    nb, lb, d = x_ref.shape
    hd = HEAD_DIM
    qk = HEADS * hd
    w3 = 3 * qk
    t = pl.program_id(1)
    parts = _row_parts(nb, lb, ROW_PARTS)
    shape = lambda bs, rs: (bs.stop - bs.start, rs.stop - rs.start)

    @pl.when(t == 0)
    def _():
        xbuf[:, PAD_ROWS - (GDN_CONV - 1):PAD_ROWS, :] = cbuf_ref[...]

    @pl.when(t > 0)
    def _():
        xbuf[:, 0:PAD_ROWS, :] = xbuf[:, lb:lb + PAD_ROWS, :]

    hbs = [_flat(_rms(x_ref[bs, rs, :], g_ref[...]) * (1.0 + sc_ref[bs]) + sh_ref[bs]).astype(BF16)
           for bs, rs in parts]
    raws = [jnp.dot(hb, w_ref[:, 0:w3], preferred_element_type=F32) for hb in hbs]
    for (bs, rs), raw in zip(parts, raws):
        xbuf[bs, PAD_ROWS + rs.start:PAD_ROWS + rs.stop, :] = raw.reshape(shape(bs, rs) + (w3,))
    cbnew_ref[...] = xbuf[:, lb + PAD_ROWS - (GDN_CONV - 1):lb + PAD_ROWS, :]

    cw = cw_ref[...]
    for (bs, rs), hb, raw in zip(parts, hbs, raws):
        pb, pr = shape(bs, rs)
        y = cb_ref[...] + raw.reshape(pb, pr, w3) * cw[GDN_CONV - 1:GDN_CONV]
        for j in range(GDN_CONV - 1):
            s = PAD_ROWS - (GDN_CONV - 1) + j
            y = y + xbuf[bs, s + rs.start:s + rs.stop, :] * cw[j:j + 1]
        a = _silu(y)
        for h in range(2 * HEADS):
            hs = slice(h * hd, (h + 1) * hd)
            v = a[:, :, hs]
            scale = lax.rsqrt(jnp.sum(v * v, axis=-1, keepdims=True) + EPS)
            qkv_ref[bs, rs, hs] = v * (scale * (hd ** -0.5) if h < HEADS else scale)
        qkv_ref[bs, rs, 2 * qk:w3] = a[:, :, 2 * qk:w3]
        gt_ref[bs, rs, :] = jnp.dot(hb, w_ref[:, w3:w3 + qk], preferred_element_type=F32).reshape(pb, pr, qk)
        ba_ref[bs, rs, :] = jnp.dot(hb, w_ref[:, w3 + qk:], preferred_element_type=F32).reshape(pb, pr, hd)


def _inproj_gdn_call(x, mod, gain, w, j, cw, cb, cbuf, nb, lb):
    bsz, seq, d = x.shape
    ntot = w.shape[2]
    qk = HEADS * HEAD_DIM
    w3 = 3 * qk
    row = lambda k: pl.BlockSpec((nb, 1, d), lambda b, t, k=k: (b, 0, k))
    tok = lambda n: pl.BlockSpec((nb, lb, n), lambda b, t: (b, t, 0))
    taps = GDN_CONV - 1
    return pl.pallas_call(
        _inproj_gdn_kernel,
        grid=(bsz // nb, seq // lb),
        in_specs=[tok(d), row(0), row(1),
                  pl.BlockSpec((1, d), lambda b, t: (0, 0)),
                  _layer_weight(d, ntot, j),
                  pl.BlockSpec((GDN_CONV, w3), lambda b, t: (0, 0)),
                  pl.BlockSpec((1, w3), lambda b, t: (0, 0)),
                  pl.BlockSpec((None, nb, taps, w3), lambda b, t: (j, b, 0, 0))],
        out_specs=[tok(w3), tok(qk), tok(HEAD_DIM), pl.BlockSpec((nb, taps, w3), lambda b, t: (b, 0, 0))],
        out_shape=[jax.ShapeDtypeStruct((bsz, seq, w3), F32), jax.ShapeDtypeStruct((bsz, seq, qk), F32),
                   jax.ShapeDtypeStruct((bsz, seq, HEAD_DIM), F32),
                   jax.ShapeDtypeStruct((bsz, taps, w3), F32)],
        scratch_shapes=[pltpu.VMEM((nb, lb + PAD_ROWS, w3), F32)],
        compiler_params=_params("arbitrary", "arbitrary"),
        name="inproj_gdn",
    )(x, mod, mod, gain.reshape(1, d), w, cw, cb.reshape(1, w3), cbuf)


def _inproj_call(x, mod, gain, w, j, splits, nb, lb):
    bsz, seq, d = x.shape
    ntot = w.shape[2]
    row = lambda k: pl.BlockSpec((nb, 1, d), lambda b, t, k=k: (b, 0, k))
    return pl.pallas_call(
        functools.partial(_inproj_kernel, splits=splits),
        grid=(bsz // nb, seq // lb),
        in_specs=[pl.BlockSpec((nb, lb, d), lambda b, t: (b, t, 0)),
                  row(0), row(1),
                  pl.BlockSpec((1, d), lambda b, t: (0, 0)),
                  _layer_weight(d, ntot, j)],
        out_specs=[pl.BlockSpec((nb, lb, n), lambda b, t: (b, t, 0)) for n in splits],
        out_shape=[jax.ShapeDtypeStruct((bsz, seq, n), F32) for n in splits],
        compiler_params=_params("arbitrary", "arbitrary"),
        name="inproj",
    )(x, mod, mod, gain.reshape(1, d), w)


def _ffn_kernel(og_ref, x_ref, g1_ref, sh2_ref, sc2_ref, g2_ref, npost_ref, npre_ref, npostf_ref,
                wout_ref, wg_ref, wu_ref, cw_ref, cb_ref, wd_ref, fbuf_ref,
                xo_ref, fbnew_ref, gbuf):
    nb, lb, d = x_ref.shape
    f = wg_ref.shape[1]
    t = pl.program_id(1)
    parts = _row_parts(nb, lb, ROW_PARTS)
    shape = lambda bs, rs: (bs.stop - bs.start, rs.stop - rs.start)
    flat = _flat

    @pl.when(t == 0)
    def _():
        gbuf[:, PAD_ROWS - (FFN_CONV - 1):PAD_ROWS, :] = fbuf_ref[...]

    @pl.when(t > 0)
    def _():
        gbuf[:, 0:PAD_ROWS, :] = gbuf[:, lb:lb + PAD_ROWS, :]

    mix = [jnp.dot(flat(og_ref[bs, rs, :]).astype(BF16), wout_ref[...], preferred_element_type=F32)
           for bs, rs in parts]
    x1, hb = [], []
    for (bs, rs), mx in zip(parts, mix):
        pb, pr = shape(bs, rs)
        xa = x_ref[bs, rs, :] + (1.0 + g1_ref[bs]) * _rms(mx.reshape(pb, pr, d), npost_ref[...])
        x1.append(xa)
        hb.append(flat(_rms(xa, npre_ref[...]) * (1.0 + sc2_ref[bs]) + sh2_ref[bs]).astype(BF16))
    gt = [jnp.dot(h, wg_ref[...], preferred_element_type=F32) for h in hb]
    up = [jnp.dot(h, wu_ref[...], preferred_element_type=F32) for h in hb]
    for (bs, rs), g in zip(parts, gt):
        pb, pr = shape(bs, rs)
        gbuf[bs, PAD_ROWS + rs.start:PAD_ROWS + rs.stop, :] = g.reshape(pb, pr, f)
    fbnew_ref[...] = gbuf[:, lb + PAD_ROWS - (FFN_CONV - 1):lb + PAD_ROWS, :]

    cw = cw_ref[...]
    out = []
    for (bs, rs), g, u in zip(parts, gt, up):
        pb, pr = shape(bs, rs)
        y = cb_ref[...] + g.reshape(pb, pr, f) * cw[FFN_CONV - 1:FFN_CONV]
        for j in range(FFN_CONV - 1):
            s = PAD_ROWS - (FFN_CONV - 1) + j
            y = y + gbuf[bs, s + rs.start:s + rs.stop, :] * cw[j:j + 1]
        act = flat(_silu(y)) * u
        out.append(jnp.dot(act.astype(BF16), wd_ref[...], preferred_element_type=F32))
    for (bs, rs), xa, o in zip(parts, x1, out):
        pb, pr = shape(bs, rs)
        xo_ref[bs, rs, :] = xa + (1.0 + g2_ref[bs]) * _rms(o.reshape(pb, pr, d), npostf_ref[...])


def _layer_weight(rows, cols, layer, col_block=0):
    return pl.BlockSpec((None, rows, cols), lambda b, t: (layer, 0, col_block), pipeline_mode=pl.Buffered(1))


def _ffn_call(og, x, mod, npost, npre, npostf, wout, jw, wgu, cw, cb, wd, layer, fbuf, nb, lb):
    bsz, seq, d = x.shape
    f = wd.shape[1]
    row = lambda k: pl.BlockSpec((nb, 1, d), lambda b, t, k=k: (b, 0, k))
    vec = lambda n: pl.BlockSpec((1, n), lambda b, t: (0, 0))
    tok = pl.BlockSpec((nb, lb, d), lambda b, t: (b, t, 0))
    taps = FFN_CONV - 1
    return pl.pallas_call(
        _ffn_kernel,
        grid=(bsz // nb, seq // lb),
        in_specs=[tok, tok, row(2), row(3), row(4), row(5), vec(d), vec(d), vec(d),
                  _layer_weight(d, d, jw), _layer_weight(d, f, layer, 0), _layer_weight(d, f, layer, 1),
                  pl.BlockSpec((FFN_CONV, f), lambda b, t: (0, 0)), vec(f), _layer_weight(f, d, layer),
                  pl.BlockSpec((None, nb, taps, f), lambda b, t: (layer, b, 0, 0))],
        out_specs=[tok, pl.BlockSpec((nb, taps, f), lambda b, t: (b, 0, 0))],
        out_shape=[jax.ShapeDtypeStruct((bsz, seq, d), F32),
                   jax.ShapeDtypeStruct((bsz, taps, f), F32)],
        scratch_shapes=[pltpu.VMEM((nb, lb + PAD_ROWS, f), F32)],
        compiler_params=_params("arbitrary", "arbitrary"),
        name="ffn_block",
    )(og, x, mod, mod, mod, mod, npost.reshape(1, d), npre.reshape(1, d), npostf.reshape(1, d),
      wout, wgu, wgu, cw, cb.reshape(1, f), wd, fbuf)


def _seq_masks(nsq, lc):
    c = nsq * lc
    i = np.arange(c)[:, None]
    j = np.arange(c)[None, :]
    same = (i // lc) == (j // lc)
    return i, j, same


def _gdn_consts(nsq, lc):
    i, j, same = _seq_masks(nsq, lc)
    return jnp.asarray((same & (j <= i)).astype(np.float32), BF16)


def _hgrn_levels(lc):
    s, out = lc, []
    while s >= 4:
        out.append(s)
        s //= 2
    return out


def _hgrn_consts(nsq, lc):
    i, j, same = _seq_masks(nsq, lc)
    mats = [same & (j <= i), same & (j > i)]
    for s in _hgrn_levels(lc):
        mid = (i // s) * s + s // 2
        mats.append(np.where(i >= mid, (j >= mid) & (j <= i), (j > i) & (j < mid)))
    return jnp.asarray(np.concatenate(mats, axis=0).astype(np.float32), BF16)


def _state_index(ch, n, nsq, lc, lr):
    return ((ch * nsq + n) * lc) // lr


def _store_heads(og_ref, ch, nsq, lc, val):
    for h in range(HEADS):
        hs = slice(h * HEAD_DIM, (h + 1) * HEAD_DIM)
        if og_ref.shape[0] == 1:
            og_ref[0, ch * CHUNK:(ch + 1) * CHUNK, hs] = val[h]
        else:
            og_ref[ch * nsq:(ch + 1) * nsq, :, hs] = val[h].reshape(nsq, lc, HEAD_DIM)


def _gdn_kernel(qkv_ref, gt_ref, ba_ref, s0_ref, alog_ref, dtb_ref, ng_ref, tri_ref, *rest, nch, nsq, lc):
    direct = nsq > 1
    og_ref, snew_ref = rest[-2:] if direct else rest[-3:-1]
    s_scr = None if direct else rest[-1]
    ns, lr, w3 = qkv_ref.shape
    r = ns * lr
    hd = HEAD_DIM
    qk = HEADS * hd
    t = pl.program_id(1)
    last = pl.num_programs(1) - 1

    if not direct:
        @pl.when(t == 0)
        def _():
            s_scr[...] = s0_ref[...]

    qkv = qkv_ref[...].reshape(r, w3)
    gt = gt_ref[...].reshape(r, qk)

    ba = ba_ref[...].reshape(r, hd)
    beta = _sigmoid(ba)
    g = -jnp.exp(alog_ref[...]) * _softplus(ba + dtb_ref[...])
    chunks = [slice(ch * CHUNK, (ch + 1) * CHUNK) for ch in range(nch)]
    gcum = [_dot_sel(tri_ref[...], g[sl], 3) for sl in chunks]
    per_item = lambda f: jnp.concatenate([f(ch, h)[None] for ch in range(nch) for h in range(HEADS)], axis=0)
    gc = per_item(lambda ch, h: gcum[ch][:, HEADS + h:HEADS + h + 1])
    bcol = per_item(lambda ch, h: beta[chunks[ch], h:h + 1])

    q = _items(qkv, nch, 0, hd)
    k = _items(qkv, nch, qk, hd)
    v = _items(qkv, nch, 2 * qk, hd)
    eg = jnp.exp(gc)
    kb = k * bcol
    qg = q * eg
    rhs = jnp.concatenate([v * bcol, kb * eg], axis=-1).astype(BF16)
    gate = _silu(_items(gt, nch, 0, hd))
    rid = _idiv(lax.broadcasted_iota(jnp.int32, (1, CHUNK, 1), 1), lc)

    lane = lax.broadcasted_iota(jnp.int32, (1, 1, 2 * CHUNK), 2)
    col = lane & (CHUNK - 1)
    row = lax.broadcasted_iota(jnp.int32, (1, CHUNK, 1), 1)
    same = _idiv(row, lc) == _idiv(col, lc)
    incl = same & (col <= row)
    strict = same & (col < row)
    gcp = jnp.where(lane < CHUNK, _even(gc), _odd(gc))
    grp = jnp.sum(jnp.where(col == row, gcp, 0.0), axis=1, keepdims=True)
    decay = jnp.where(incl, jnp.exp(jnp.where(incl, gcp - grp, 0.0)), 0.0)
    kst = _stack(k.astype(BF16))
    kk_qk = _bmm_nt(jnp.concatenate([_side(kb.astype(BF16)), _side(q.astype(BF16))], axis=1), kst)
    a = jnp.where(strict, kk_qk[:, :CHUNK] * decay, 0.0)
    attn = jnp.where(incl, kk_qk[:, CHUNK:] * decay, 0.0)
    tinv = _packed_unit_lower_inverse(a, min(16, lc), lc, _idiv(lane, CHUNK), row, col)
    th, tl = _split2(tinv)
    sol = _bmm(jnp.concatenate([th, tl], axis=1), _stack(rhs))
    sol = sol[:, :CHUNK] + sol[:, CHUNK:]
    uv, w = _halves(sol, 0, hd), _halves(sol, hd, hd)

    for ch in range(nch):
        it = slice(ch * HEADS, (ch + 1) * HEADS)
        pr = slice(ch * HEADS // 2, (ch + 1) * HEADS // 2)
        us, os_ = [], []
        for n in range(nsq):
            sl = slice(n * lc, (n + 1) * lc)
            st = (s0_ref if direct else s_scr)[_state_index(ch, n, nsq, lc, lr)]
            ws = _bmm(jnp.concatenate([w[it, sl], qg[it, sl]], axis=1), st)
            us.append(uv[it, sl] - ws[:, :lc])
            os_.append(ws[:, lc:])
        u = us[0] if nsq == 1 else jnp.concatenate(us, axis=1)
        o = (os_[0] if nsq == 1 else jnp.concatenate(os_, axis=1)) + _halves(_bmm(attn[pr], _stack(u.astype(BF16))), 0, hd)
        for n in range(nsq):
            si = _state_index(ch, n, nsq, lc, lr)
            gl = gc[it, (n + 1) * lc - 1:(n + 1) * lc, :]
            if nsq > 1:
                kd = jnp.where(rid == n, k[it] * jnp.exp(jnp.where(rid == n, gl - gc[it], 0.0)), 0.0)
            else:
                kd = k[it] * jnp.exp(gl - gc[it])
            s_new = (s0_ref if direct else s_scr)[si] * jnp.exp(gl) + _bmm(jnp.swapaxes(kd, 1, 2), u)
            if direct:
                _write_state(snew_ref, si, s_new)
            else:
                s_scr[si] = s_new
        _store_heads(og_ref, ch, nsq, lc, _rms(o, ng_ref[...]) * gate[it])

    if not direct:
        @pl.when(t == last)
        def _():
            for n in range(ns):
                _write_state(snew_ref, n, s_scr[n])


def _stacked_state(states, j, prev, ns):
    tail = states.shape[2:]
    in_spec = pl.BlockSpec((None, ns) + tail, lambda b, t: (j, b, 0, 0, 0))
    if prev is None:
        out_spec = pl.BlockSpec((states.shape[0], ns) + tail, lambda b, t: (0, b, 0, 0, 0))
        extra_in, extra_specs = [], []
    else:
        out_spec = in_spec
        extra_in, extra_specs = [prev], [pl.BlockSpec(memory_space=pl.ANY)]
    return in_spec, out_spec, extra_in, extra_specs, jax.ShapeDtypeStruct(states.shape, states.dtype)


def _write_state(snew_ref, n, value):
    if len(snew_ref.shape) == 5:
        for m in range(snew_ref.shape[0]):
            snew_ref[m, n] = value
    else:
        snew_ref[n] = value


def _gdn_call(qkv, gt, ba, states, j, prev, alog, dtb, ng, nch, nsq, lc):
    assert 2 * CHUNK == LANES and HEADS % 2 == 0, "two heads' (CHUNK, CHUNK) matrices fill one lane tile"
    bsz, seq, w3 = qkv.shape
    qk = w3 // 3
    hd = HEAD_DIM
    tri = _gdn_consts(nsq, lc)
    ns, lr = (1, nch * CHUNK) if nsq == 1 else (nch * nsq, lc)
    tok = lambda n: pl.BlockSpec((ns, lr, n), lambda b, t: (b, t, 0))
    vec = lambda n: pl.BlockSpec((1, n), lambda b, t: (0, 0))
    state, state_out, extra_in, extra_specs, state_shape = _stacked_state(states, j, prev, ns)
    inputs = [qkv, gt, ba, states, alog, dtb, ng.reshape(1, hd), tri] + extra_in
    return pl.pallas_call(
        functools.partial(_gdn_kernel, nch=nch, nsq=nsq, lc=lc),
        grid=(bsz // ns, seq // lr),
        in_specs=[tok(w3), tok(qk), tok(hd), state, vec(hd), vec(hd), vec(hd),
                  pl.BlockSpec((CHUNK, CHUNK), lambda b, t: (0, 0))] + extra_specs,
        out_specs=[tok(qk), state_out],
        out_shape=[jax.ShapeDtypeStruct((bsz, seq, qk), F32), state_shape],
        input_output_aliases={len(inputs) - 1: 1} if extra_in else {},
        scratch_shapes=[] if nsq > 1 else [pltpu.VMEM((ns, HEADS, hd, hd), F32)],
        compiler_params=_params("arbitrary", "arbitrary"),
        name="gdn_core",
    )(*inputs)


def _hgrn_kernel(proj_ref, s0_ref, lbraw_ref, ng_ref, cm_ref, *rest, nch, nsq, lc, layer):
    direct = nsq > 1
    og_ref, snew_ref = rest[-2:] if direct else rest[-3:-1]
    st_scr = None if direct else rest[-1]
    ns, lr, w4 = proj_ref.shape
    r = ns * lr
    hd = HEAD_DIM
    qf = HEADS * hd
    t = pl.program_id(1)
    last = pl.num_programs(1) - 1

    if not direct:
        @pl.when(t == 0)
        def _():
            for n in range(ns):
                st_scr[n] = jnp.swapaxes(s0_ref[n], 1, 2)

    lbraw = lbraw_ref[...]
    e = jnp.exp(lbraw - jnp.max(lbraw, axis=0, keepdims=True))
    p = e / jnp.sum(e, axis=0, keepdims=True)
    lb = jnp.zeros((1, qf), F32)
    for m in range(1, layer + 1):
        lb = lb + p[m:m + 1, :]

    proj = proj_ref[...].reshape(r, w4)
    fr = proj[:, qf:2 * qf]
    efr = jnp.exp(-jnp.abs(fr))
    big = 1.0 / (1.0 + efr)
    small = efr * big
    sig_pos = jnp.where(fr >= 0.0, big, small)
    sig_neg = jnp.where(fr >= 0.0, small, big)
    forget = jnp.maximum(lb, LB_FLOOR) + (1.0 - lb) * sig_pos
    logf = jnp.log(forget)

    levels = _hgrn_levels(lc)
    nmat = 2 + len(levels)
    cm = cm_ref[...]
    ex = jnp.concatenate([jnp.exp(_dot_sel(cm, logf[ch * CHUNK:(ch + 1) * CHUNK], 2)) for ch in range(nch)], axis=0)
    factor = lambda m: jnp.concatenate(
        [ex[(ch * nmat + m) * CHUNK:(ch * nmat + m + 1) * CHUNK, h * hd:(h + 1) * hd][None]
         for ch in range(nch) for h in range(HEADS)], axis=0)

    q = _silu(_items(proj, nch, 0, hd))
    k = _items((1.0 - lb) * sig_neg, nch, 0, hd)
    v = _items(proj, nch, 2 * qf, hd)
    gate = _silu(_items(proj, nch, 3 * qf, hd))

    row = lax.broadcasted_iota(jnp.int32, (CHUNK, CHUNK), 0)
    col = lax.broadcasted_iota(jnp.int32, (CHUNK, CHUNK), 1)
    qb, kb = q.astype(BF16), k.astype(BF16)
    attn = jnp.where((row == col)[None], _bmm_nt(qb, kb), 0.0)
    odd = (lax.broadcasted_iota(jnp.int32, (1, CHUNK, 1), 1) & 1) == 1
    factors = [jnp.where(odd, _items(forget, nch, 0, hd), 1.0)] + [factor(2 + li) for li in range(len(levels))]
    for s, fl in zip([2] + levels, factors):
        mask = (_idiv(row, s) == _idiv(col, s)) & ((row & (s - 1)) >= s // 2) & ((col & (s - 1)) < s // 2)
        flb = fl.astype(BF16)
        attn = jnp.where(mask[None], _bmm_nt(qb * flb, kb * flb), attn)
    eg = factor(0)
    qe = q * eg
    kd = k * factor(1)
    cid = _idiv(lax.broadcasted_iota(jnp.int32, (1, 1, CHUNK), 2), lc)

    for ch in range(nch):
        it = slice(ch * HEADS, (ch + 1) * HEADS)
        vt = jnp.swapaxes(v[it], 1, 2)
        os_ = []
        for n in range(nsq):
            sl = slice(n * lc, (n + 1) * lc)
            si = _state_index(ch, n, nsq, lc, lr)
            st = jnp.swapaxes(s0_ref[si], 1, 2) if direct else st_scr[si]
            os_.append(_bmm_nt(qe[it, sl], st))
            vtn = vt if nsq == 1 else jnp.where(cid == n, vt, 0.0)
            egl = eg[it, (n + 1) * lc - 1:(n + 1) * lc, :]
            st_new = st * egl + _bmm(vtn, kd[it])
            if direct:
                _write_state(snew_ref, si, jnp.swapaxes(st_new, 1, 2))
            else:
                st_scr[si] = st_new
        o = (os_[0] if nsq == 1 else jnp.concatenate(os_, axis=1)) + _bmm(attn[it], v[it])
        _store_heads(og_ref, ch, nsq, lc, _rms(o, ng_ref[...]) * gate[it])

    if not direct:
        @pl.when(t == last)
        def _():
            for n in range(ns):
                _write_state(snew_ref, n, jnp.swapaxes(st_scr[n], 1, 2))


def _hgrn_call(proj, states, j, prev, lbraw, ng, nch, nsq, lc):
    bsz, seq, w4 = proj.shape
    qf = w4 // 4
    hd = HEAD_DIM
    cm = _hgrn_consts(nsq, lc)
    ns, lr = (1, nch * CHUNK) if nsq == 1 else (nch * nsq, lc)
    tok = lambda n: pl.BlockSpec((ns, lr, n), lambda b, t: (b, t, 0))
    state, state_out, extra_in, extra_specs, state_shape = _stacked_state(states, j, prev, ns)
    inputs = [proj, states, lbraw, ng.reshape(1, hd), cm] + extra_in
    return pl.pallas_call(
        functools.partial(_hgrn_kernel, nch=nch, nsq=nsq, lc=lc, layer=j),
        grid=(bsz // ns, seq // lr),
        in_specs=[tok(w4), state,
                  pl.BlockSpec(lbraw.shape, lambda b, t: (0, 0)),
                  pl.BlockSpec((1, hd), lambda b, t: (0, 0)),
                  pl.BlockSpec(cm.shape, lambda b, t: (0, 0))] + extra_specs,
        out_specs=[tok(qf), state_out],
        out_shape=[jax.ShapeDtypeStruct((bsz, seq, qf), F32), state_shape],
        input_output_aliases={len(inputs) - 1: 1} if extra_in else {},
        scratch_shapes=[] if nsq > 1 else [pltpu.VMEM((ns, HEADS, hd, hd), F32)],
        compiler_params=_params("arbitrary", "arbitrary"),
        name="hgrn_core",
    )(*inputs)


def _prep_weights(gdn_w_in, gdn_a_log, gdn_dt_bias, hgrn_w_in, gdn_w_out, hgrn_w_out, ffn_w_gu, ffn_w_down):
    pad = HEAD_DIM - 2 * HEADS
    w_gdn = jnp.pad(gdn_w_in, ((0, 0), (0, 0), (0, pad))).astype(BF16)
    gate_pad = lambda a: jnp.pad(a, ((0, 0), (HEADS, HEAD_DIM - 2 * HEADS)))[:, None, :]
    return dict(
        w_gdn=w_gdn, alog=gate_pad(gdn_a_log), dtb=gate_pad(gdn_dt_bias),
        w_hgrn=hgrn_w_in.astype(BF16), gdn_w_out=gdn_w_out.astype(BF16), hgrn_w_out=hgrn_w_out.astype(BF16),
        wgu=ffn_w_gu.astype(BF16), wd=ffn_w_down.astype(BF16))


def _trunk(x, mod, s_gdn, s_gconv, s_hgrn, s_fconv, wts, p, nb, lb, nsq, lc):
    depth = mod.shape[0]
    bsz, seq, _ = x.shape
    qk = HEADS * HEAD_DIM
    def chunks_per_step(limit, out_layers):
        if nsq == 1:
            return min(limit, seq // CHUNK)
        state_bytes = (2 + 2 * out_layers) * nsq * HEADS * HEAD_DIM * HEAD_DIM * 4
        return min(limit, max(1, min(bsz // nsq, STATE_VMEM_BUDGET // state_bytes)))

    new_gdn, new_gconv, new_hgrn, new_fconv = None, [], None, []
    for layer in range(depth):
        modl = mod[layer][:, None, :]
        j = layer // N_MIXERS
        if layer % N_MIXERS == 0:
            qkv, gt, ba, cb_new = _inproj_gdn_call(x, modl, wts["norm_pre_mix"][layer], p["w_gdn"], j,
                                                   wts["gdn_conv_w"][j], wts["gdn_conv_b"][j], s_gconv, nb, lb)
            og, new_gdn = _gdn_call(qkv, gt, ba, s_gdn, j, new_gdn, p["alog"][j], p["dtb"][j],
                                    wts["gdn_norm"][j],
                                    chunks_per_step(GDN_CHUNKS, s_gdn.shape[0] if new_gdn is None else 1), nsq, lc)
            new_gconv.append(cb_new)
            w_out = p["gdn_w_out"]
        else:
            (proj,) = _inproj_call(x, modl, wts["norm_pre_mix"][layer], p["w_hgrn"], j, (4 * qk,), nb, lb)
            og, new_hgrn = _hgrn_call(proj, s_hgrn, j, new_hgrn, wts["hgrn_lb"], wts["hgrn_norm"][j],
                                      chunks_per_step(HGRN_CHUNKS, s_hgrn.shape[0] if new_hgrn is None else 1),
                                      nsq, lc)
            w_out = p["hgrn_w_out"]
        x, fb_new = _ffn_call(og, x, modl, wts["norm_post_mix"][layer], wts["norm_pre_ffn"][layer],
                              wts["norm_post_ffn"][layer], w_out, j, p["wgu"],
                              wts["ffn_conv_w"][layer], wts["ffn_conv_b"][layer], p["wd"], layer, s_fconv, nb, lb)
        new_fconv.append(fb_new)
    return x, new_gdn, jnp.stack(new_gconv), new_hgrn, jnp.stack(new_fconv)


def kernel(x_prompt, x_sample, state_gdn, state_gdn_conv, state_hgrn, state_ffn_conv, c_prompt, c_sample, ada_w, ada_b, norm_pre_mix, norm_post_mix, norm_pre_ffn, norm_post_ffn, gdn_w_in, gdn_conv_w, gdn_conv_b, gdn_a_log, gdn_dt_bias, gdn_norm, gdn_w_out, hgrn_lb, hgrn_w_in, hgrn_norm, hgrn_w_out, ffn_w_gu, ffn_conv_w, ffn_conv_b, ffn_w_down):
    wts = dict(norm_pre_mix=norm_pre_mix, norm_post_mix=norm_post_mix, norm_pre_ffn=norm_pre_ffn,
               norm_post_ffn=norm_post_ffn, gdn_conv_w=gdn_conv_w, gdn_conv_b=gdn_conv_b, gdn_norm=gdn_norm,
               hgrn_lb=hgrn_lb, hgrn_norm=hgrn_norm, ffn_conv_w=ffn_conv_w, ffn_conv_b=ffn_conv_b)
    p = _prep_weights(gdn_w_in, gdn_a_log, gdn_dt_bias, hgrn_w_in, gdn_w_out, hgrn_w_out, ffn_w_gu, ffn_w_down)

    bp, sp, _ = x_prompt.shape
    bs, ss, _ = x_sample.shape
    mod = _mod_call(jnp.concatenate([c_prompt, c_sample], axis=0), ada_w, ada_b)
    mod_p, mod_s = mod[:, :bp], mod[:, bp:]

    dt = x_prompt.dtype
    zeros = lambda a: jnp.zeros((a.shape[0], bp) + a.shape[2:], dt)
    y_p, p_gdn, p_gconv, p_hgrn, p_fconv = _trunk(
        x_prompt, mod_p, zeros(state_gdn), zeros(state_gdn_conv), zeros(state_hgrn), zeros(state_ffn_conv),
        wts, p, nb=1, lb=min(TILE_ROWS, sp), nsq=1, lc=min(CHUNK, sp))
    y_s, s_gdn, s_gconv, s_hgrn, s_fconv = _trunk(
        x_sample, mod_s, state_gdn, state_gdn_conv, state_hgrn, state_ffn_conv,
        wts, p, nb=min(bs, SHORT_TILE_ROWS // ss), lb=ss, nsq=min(bs, CHUNK // ss), lc=ss)
    return (y_p, y_s, p_gdn, p_gconv, p_hgrn, p_fconv, s_gdn, s_gconv, s_hgrn, s_fconv)
```

```python
import functools

import numpy as np
import jax
import jax.numpy as jnp
from jax import lax
from jax.experimental import pallas as pl
from jax.experimental.pallas import tpu as pltpu

F32 = jnp.float32
BF16 = jnp.bfloat16
EPS = 1e-6
LB_FLOOR = 1e-30
N_MIXERS = 2
LANES = 128
HEADS = 8
HEAD_DIM = 128
GDN_CONV = 4
FFN_CONV = 3
CHUNK = 64
PAD_ROWS = 8
TILE_ROWS = 512
SHORT_TILE_ROWS = 256
GDN_CHUNKS = 8
HGRN_CHUNKS = 8
ROW_PARTS = 2
VMEM_LIMIT = 56 * 1024 * 1024
STATE_VMEM_BUDGET = 32 * 1024 * 1024


def _params(*sem, fuse=None):
    return pltpu.CompilerParams(dimension_semantics=sem, vmem_limit_bytes=VMEM_LIMIT, allow_input_fusion=fuse)


def _sigmoid(x):
    return jax.nn.sigmoid(x)


def _silu(x):
    hx = 0.5 * x
    return hx + hx * jnp.tanh(hx)


def _softplus(x):
    return jnp.maximum(x, 0.0) + jnp.log1p(jnp.exp(-jnp.abs(x)))


def _rms(x, g):
    return x * lax.rsqrt(jnp.mean(x * x, axis=-1, keepdims=True) + EPS) * g


def _dot(a, b):
    return jnp.dot(a.astype(BF16), b.astype(BF16), preferred_element_type=F32)


def _bmm(a, b):
    return lax.dot_general(a.astype(BF16), b.astype(BF16), (((2,), (1,)), ((0,), (0,))),
                           preferred_element_type=F32)


def _bmm_nt(a, b):
    return lax.dot_general(a.astype(BF16), b.astype(BF16), (((2,), (2,)), ((0,), (0,))),
                           preferred_element_type=F32)


def _split2(x):
    hi = x.astype(BF16)
    lo = (x - hi.astype(F32)).astype(BF16)
    return hi, lo


def _even(x):
    return jnp.concatenate([x[i:i + 1] for i in range(0, x.shape[0], 2)], axis=0)


def _odd(x):
    return jnp.concatenate([x[i:i + 1] for i in range(1, x.shape[0], 2)], axis=0)


def _side(x):
    return jnp.concatenate([_even(x), _odd(x)], axis=-1)


def _stack(x):
    e, o = _even(x), _odd(x)
    return jnp.concatenate([jnp.concatenate([e, jnp.zeros_like(e)], axis=-1),
                            jnp.concatenate([jnp.zeros_like(o), o], axis=-1)], axis=1)


def _halves(x, off, width):
    half = x.shape[-1] // 2
    return jnp.concatenate([x[p:p + 1, :, s * half + off:s * half + off + width]
                            for p in range(x.shape[0]) for s in range(2)], axis=0)


def _pair_product(lhs, rhs, grp):
    c = rhs.shape[1]
    n = len(lhs)
    bd = lambda x: jnp.concatenate([jnp.where(grp == m, x, jnp.zeros_like(x))
                                    for m in range(x.shape[-1] // c)], axis=1)
    rh, rl = _split2(rhs)
    pieces = [_split2(x) for x in lhs]
    his = [hi for hi, _ in pieces]
    los = [lo for _, lo in pieces]
    top = _bmm(jnp.concatenate(his + los, axis=1), bd(rh))
    bot = _bmm(his[0] if n == 1 else jnp.concatenate(his, axis=1), bd(rl))
    return [top[:, i * c:(i + 1) * c] + top[:, (n + i) * c:(n + i + 1) * c] + bot[:, i * c:(i + 1) * c]
            for i in range(n)]


def _dot_sel(sel, x, pieces):
    acc = None
    r = x
    for i in range(pieces):
        part = r.astype(BF16)
        if i + 1 < pieces:
            r = r - part.astype(F32)
        term = jnp.dot(sel, part, preferred_element_type=F32)
        acc = term if acc is None else acc + term
    return acc


def _idiv(x, p2):
    return x >> (p2.bit_length() - 1)


def _packed_unit_lower_inverse(a, block, span, grp, row, col):
    eye = (col == row).astype(F32)
    if block >= span:
        return _packed_neumann(eye, a, span, grp)
    diag = _idiv(row, block) == _idiv(col, block)
    inv = _packed_neumann(eye, jnp.where(diag, a, 0.0), block, grp)
    (low,) = _pair_product([inv], jnp.where(diag, 0.0, a), grp)
    blk = _packed_neumann(eye, low, span // block, grp)
    (res,) = _pair_product([blk], inv, grp)
    return res


def _packed_neumann(eye, d, order, grp):
    p = eye - d
    if order <= 2:
        return p
    (dp,) = _pair_product([d], d, grp)
    k = 4
    while k < order:
        pp, dp_next = _pair_product([p, dp], dp, grp)
        p, dp = p + pp, dp_next
        k *= 2
    (pp,) = _pair_product([p], dp, grp)
    return p + pp


def _items(x, nch, base, width):
    return jnp.concatenate(
        [x[ch * CHUNK:(ch + 1) * CHUNK, base + h * width:base + (h + 1) * width][None]
         for ch in range(nch) for h in range(HEADS)], axis=0)


def _mod_kernel(c_ref, w_ref, b_ref, o_ref):
    cs = _silu(c_ref[...])
    o_ref[0] = _dot(cs, w_ref[0]) + b_ref[0]


def _mod_call(c_all, ada_w, ada_b):
    depth, d, n = ada_w.shape
    rows = c_all.shape[0]
    tn = n // 4
    return pl.pallas_call(
        _mod_kernel,
        grid=(depth, n // tn),
        in_specs=[pl.BlockSpec((rows, d), lambda l, j: (0, 0)),
                  pl.BlockSpec((1, d, tn), lambda l, j: (l, 0, j)),
                  pl.BlockSpec((1, 1, tn), lambda l, j: (l, 0, j))],
        out_specs=pl.BlockSpec((1, rows, tn), lambda l, j: (l, 0, j)),
        out_shape=jax.ShapeDtypeStruct((depth, rows, n), F32),
        compiler_params=_params("arbitrary", "arbitrary"),
        name="adaln_mod",
    )(c_all, ada_w, ada_b.reshape(depth, 1, n))


def _row_parts(nb, lb, count):
    if nb >= count:
        return [(slice(i * nb // count, (i + 1) * nb // count), slice(0, lb)) for i in range(count)]
    return [(slice(0, nb), slice(i * lb // count, (i + 1) * lb // count)) for i in range(count)]


def _flat(v):
    return v.reshape(v.shape[0] * v.shape[1], v.shape[2])


def _inproj_kernel(x_ref, sh_ref, sc_ref, g_ref, w_ref, *out_refs, splits):
    nb, lb, d = x_ref.shape
    parts = _row_parts(nb, lb, ROW_PARTS)
    hbs = [_flat(_rms(x_ref[bs, rs, :], g_ref[...]) * (1.0 + sc_ref[bs]) + sh_ref[bs]).astype(BF16)
           for bs, rs in parts]
    off = 0
    for o_ref, n in zip(out_refs, splits):
        for (bs, rs), hb in zip(parts, hbs):
            res = jnp.dot(hb, w_ref[:, off:off + n], preferred_element_type=F32)
            o_ref[bs, rs, :] = res.reshape(bs.stop - bs.start, rs.stop - rs.start, n)
        off += n


def _inproj_gdn_kernel(x_ref, sh_ref, sc_ref, g_ref, w_ref, cw_ref, cb_ref, cbuf_ref,
                       qkv_ref, gt_ref, ba_ref, cbnew_ref, xbuf):
    nb, lb, d = x_ref.shape
    hd = HEAD_DIM
    qk = HEADS * hd
    w3 = 3 * qk
    t = pl.program_id(1)
    parts = _row_parts(nb, lb, ROW_PARTS)
    shape = lambda bs, rs: (bs.stop - bs.start, rs.stop - rs.start)

    @pl.when(t == 0)
    def _():
        xbuf[:, PAD_ROWS - (GDN_CONV - 1):PAD_ROWS, :] = cbuf_ref[...]

    @pl.when(t > 0)
    def _():
        xbuf[:, 0:PAD_ROWS, :] = xbuf[:, lb:lb + PAD_ROWS, :]

    hbs = [_flat(_rms(x_ref[bs, rs, :], g_ref[...]) * (1.0 + sc_ref[bs]) + sh_ref[bs]).astype(BF16)
           for bs, rs in parts]
    raws = [jnp.dot(hb, w_ref[:, 0:w3], preferred_element_type=F32) for hb in hbs]
    for (bs, rs), raw in zip(parts, raws):
        xbuf[bs, PAD_ROWS + rs.start:PAD_ROWS + rs.stop, :] = raw.reshape(shape(bs, rs) + (w3,))
    cbnew_ref[...] = xbuf[:, lb + PAD_ROWS - (GDN_CONV - 1):lb + PAD_ROWS, :]

    cw = cw_ref[...]
    for (bs, rs), hb, raw in zip(parts, hbs, raws):
        pb, pr = shape(bs, rs)
        y = cb_ref[...] + raw.reshape(pb, pr, w3) * cw[GDN_CONV - 1:GDN_CONV]
        for j in range(GDN_CONV - 1):
            s = PAD_ROWS - (GDN_CONV - 1) + j
            y = y + xbuf[bs, s + rs.start:s + rs.stop, :] * cw[j:j + 1]
        a = _silu(y)
        for h in range(2 * HEADS):
            hs = slice(h * hd, (h + 1) * hd)
            v = a[:, :, hs]
            scale = lax.rsqrt(jnp.sum(v * v, axis=-1, keepdims=True) + EPS)
            qkv_ref[bs, rs, hs] = v * (scale * (hd ** -0.5) if h < HEADS else scale)
        qkv_ref[bs, rs, 2 * qk:w3] = a[:, :, 2 * qk:w3]
        gt_ref[bs, rs, :] = jnp.dot(hb, w_ref[:, w3:w3 + qk], preferred_element_type=F32).reshape(pb, pr, qk)
        ba_ref[bs, rs, :] = jnp.dot(hb, w_ref[:, w3 + qk:], preferred_element_type=F32).reshape(pb, pr, hd)


def _inproj_gdn_call(x, mod, gain, w, j, cw, cb, cbuf, nb, lb):
    bsz, seq, d = x.shape
    ntot = w.shape[2]
    qk = HEADS * HEAD_DIM
    w3 = 3 * qk
    row = lambda k: pl.BlockSpec((nb, 1, d), lambda b, t, k=k: (b, 0, k))
    tok = lambda n: pl.BlockSpec((nb, lb, n), lambda b, t: (b, t, 0))
    taps = GDN_CONV - 1
    return pl.pallas_call(
        _inproj_gdn_kernel,
        grid=(bsz // nb, seq // lb),
        in_specs=[tok(d), row(0), row(1),
                  pl.BlockSpec((1, d), lambda b, t: (0, 0)),
                  _layer_weight(d, ntot, j),
                  pl.BlockSpec((GDN_CONV, w3), lambda b, t: (0, 0)),
                  pl.BlockSpec((1, w3), lambda b, t: (0, 0)),
                  pl.BlockSpec((None, nb, taps, w3), lambda b, t: (j, b, 0, 0))],
        out_specs=[tok(w3), tok(qk), tok(HEAD_DIM), pl.BlockSpec((nb, taps, w3), lambda b, t: (b, 0, 0))],
        out_shape=[jax.ShapeDtypeStruct((bsz, seq, w3), F32), jax.ShapeDtypeStruct((bsz, seq, qk), F32),
                   jax.ShapeDtypeStruct((bsz, seq, HEAD_DIM), F32),
                   jax.ShapeDtypeStruct((bsz, taps, w3), F32)],
        scratch_shapes=[pltpu.VMEM((nb, lb + PAD_ROWS, w3), F32)],
        compiler_params=_params("arbitrary", "arbitrary"),
        name="inproj_gdn",
    )(x, mod, mod, gain.reshape(1, d), w, cw, cb.reshape(1, w3), cbuf)


def _inproj_call(x, mod, gain, w, j, splits, nb, lb):
    bsz, seq, d = x.shape
    ntot = w.shape[2]
    row = lambda k: pl.BlockSpec((nb, 1, d), lambda b, t, k=k: (b, 0, k))
    return pl.pallas_call(
        functools.partial(_inproj_kernel, splits=splits),
        grid=(bsz // nb, seq // lb),
        in_specs=[pl.BlockSpec((nb, lb, d), lambda b, t: (b, t, 0)),
                  row(0), row(1),
                  pl.BlockSpec((1, d), lambda b, t: (0, 0)),
                  _layer_weight(d, ntot, j)],
        out_specs=[pl.BlockSpec((nb, lb, n), lambda b, t: (b, t, 0)) for n in splits],
        out_shape=[jax.ShapeDtypeStruct((bsz, seq, n), F32) for n in splits],
        compiler_params=_params("arbitrary", "arbitrary"),
        name="inproj",
    )(x, mod, mod, gain.reshape(1, d), w)


def _ffn_kernel(og_ref, x_ref, g1_ref, sh2_ref, sc2_ref, g2_ref, npost_ref, npre_ref, npostf_ref,
                wout_ref, wg_ref, wu_ref, cw_ref, cb_ref, wd_ref, fbuf_ref,
                xo_ref, fbnew_ref, gbuf):
    nb, lb, d = x_ref.shape
    f = wg_ref.shape[1]
    t = pl.program_id(1)
    parts = _row_parts(nb, lb, ROW_PARTS)
    shape = lambda bs, rs: (bs.stop - bs.start, rs.stop - rs.start)
    flat = _flat

    @pl.when(t == 0)
    def _():
        gbuf[:, PAD_ROWS - (FFN_CONV - 1):PAD_ROWS, :] = fbuf_ref[...]

    @pl.when(t > 0)
    def _():
        gbuf[:, 0:PAD_ROWS, :] = gbuf[:, lb:lb + PAD_ROWS, :]

    mix = [jnp.dot(flat(og_ref[bs, rs, :]).astype(BF16), wout_ref[...], preferred_element_type=F32)
           for bs, rs in parts]
    x1, hb = [], []
    for (bs, rs), mx in zip(parts, mix):
        pb, pr = shape(bs, rs)
        xa = x_ref[bs, rs, :] + (1.0 + g1_ref[bs]) * _rms(mx.reshape(pb, pr, d), npost_ref[...])
        x1.append(xa)
        hb.append(flat(_rms(xa, npre_ref[...]) * (1.0 + sc2_ref[bs]) + sh2_ref[bs]).astype(BF16))
    gt = [jnp.dot(h, wg_ref[...], preferred_element_type=F32) for h in hb]
    up = [jnp.dot(h, wu_ref[...], preferred_element_type=F32) for h in hb]
    for (bs, rs), g in zip(parts, gt):
        pb, pr = shape(bs, rs)
        gbuf[bs, PAD_ROWS + rs.start:PAD_ROWS + rs.stop, :] = g.reshape(pb, pr, f)
    fbnew_ref[...] = gbuf[:, lb + PAD_ROWS - (FFN_CONV - 1):lb + PAD_ROWS, :]

    cw = cw_ref[...]
    out = []
    for (bs, rs), g, u in zip(parts, gt, up):
        pb, pr = shape(bs, rs)
        y = cb_ref[...] + g.reshape(pb, pr, f) * cw[FFN_CONV - 1:FFN_CONV]
        for j in range(FFN_CONV - 1):
            s = PAD_ROWS - (FFN_CONV - 1) + j
            y = y + gbuf[bs, s + rs.start:s + rs.stop, :] * cw[j:j + 1]
        act = flat(_silu(y)) * u
        out.append(jnp.dot(act.astype(BF16), wd_ref[...], preferred_element_type=F32))
    for (bs, rs), xa, o in zip(parts, x1, out):
        pb, pr = shape(bs, rs)
        xo_ref[bs, rs, :] = xa + (1.0 + g2_ref[bs]) * _rms(o.reshape(pb, pr, d), npostf_ref[...])


def _layer_weight(rows, cols, layer, col_block=0):
    return pl.BlockSpec((None, rows, cols), lambda b, t: (layer, 0, col_block), pipeline_mode=pl.Buffered(1))


def _ffn_call(og, x, mod, npost, npre, npostf, wout, jw, wgu, cw, cb, wd, layer, fbuf, nb, lb):
    bsz, seq, d = x.shape
    f = wd.shape[1]
    row = lambda k: pl.BlockSpec((nb, 1, d), lambda b, t, k=k: (b, 0, k))
    vec = lambda n: pl.BlockSpec((1, n), lambda b, t: (0, 0))
    tok = pl.BlockSpec((nb, lb, d), lambda b, t: (b, t, 0))
    taps = FFN_CONV - 1
    return pl.pallas_call(
        _ffn_kernel,
        grid=(bsz // nb, seq // lb),
        in_specs=[tok, tok, row(2), row(3), row(4), row(5), vec(d), vec(d), vec(d),
                  _layer_weight(d, d, jw), _layer_weight(d, f, layer, 0), _layer_weight(d, f, layer, 1),
                  pl.BlockSpec((FFN_CONV, f), lambda b, t: (0, 0)), vec(f), _layer_weight(f, d, layer),
                  pl.BlockSpec((None, nb, taps, f), lambda b, t: (layer, b, 0, 0))],
        out_specs=[tok, pl.BlockSpec((nb, taps, f), lambda b, t: (b, 0, 0))],
        out_shape=[jax.ShapeDtypeStruct((bsz, seq, d), F32),
                   jax.ShapeDtypeStruct((bsz, taps, f), F32)],
        scratch_shapes=[pltpu.VMEM((nb, lb + PAD_ROWS, f), F32)],
        compiler_params=_params("arbitrary", "arbitrary", fuse=[i in (9, 10, 11, 14) for i in range(16)]),
        name="ffn_block",
    )(og, x, mod, mod, mod, mod, npost.reshape(1, d), npre.reshape(1, d), npostf.reshape(1, d),
      wout, wgu, wgu, cw, cb.reshape(1, f), wd, fbuf)


def _seq_masks(nsq, lc):
    c = nsq * lc
    i = np.arange(c)[:, None]
    j = np.arange(c)[None, :]
    same = (i // lc) == (j // lc)
    return i, j, same


def _gdn_consts(nsq, lc):
    i, j, same = _seq_masks(nsq, lc)
    return jnp.asarray((same & (j <= i)).astype(np.float32), BF16)


def _hgrn_levels(lc):
    s, out = lc, []
    while s >= 4:
        out.append(s)
        s //= 2
    return out


def _hgrn_consts(nsq, lc):
    i, j, same = _seq_masks(nsq, lc)
    mats = [same & (j <= i), same & (j > i)]
    for s in _hgrn_levels(lc):
        mid = (i // s) * s + s // 2
        mats.append(np.where(i >= mid, (j >= mid) & (j <= i), (j > i) & (j < mid)))
    return jnp.asarray(np.concatenate(mats, axis=0).astype(np.float32), BF16)


def _state_index(ch, n, nsq, lc, lr):
    return ((ch * nsq + n) * lc) // lr


def _store_heads(og_ref, ch, nsq, lc, val):
    for h in range(HEADS):
        hs = slice(h * HEAD_DIM, (h + 1) * HEAD_DIM)
        if og_ref.shape[0] == 1:
            og_ref[0, ch * CHUNK:(ch + 1) * CHUNK, hs] = val[h]
        else:
            og_ref[ch * nsq:(ch + 1) * nsq, :, hs] = val[h].reshape(nsq, lc, HEAD_DIM)


def _gdn_kernel(qkv_ref, gt_ref, ba_ref, s0_ref, alog_ref, dtb_ref, ng_ref, tri_ref, *rest, nch, nsq, lc):
    og_ref, snew_ref, s_scr = rest[-3:]
    ns, lr, w3 = qkv_ref.shape
    r = ns * lr
    hd = HEAD_DIM
    qk = HEADS * hd
    t = pl.program_id(1)
    last = pl.num_programs(1) - 1

    @pl.when(t == 0)
    def _():
        s_scr[...] = s0_ref[...]

    qkv = qkv_ref[...].reshape(r, w3)
    gt = gt_ref[...].reshape(r, qk)

    ba = ba_ref[...].reshape(r, hd)
    beta = _sigmoid(ba)
    g = -jnp.exp(alog_ref[...]) * _softplus(ba + dtb_ref[...])
    chunks = [slice(ch * CHUNK, (ch + 1) * CHUNK) for ch in range(nch)]
    gcum = [_dot_sel(tri_ref[...], g[sl], 3) for sl in chunks]
    per_item = lambda f: jnp.concatenate([f(ch, h)[None] for ch in range(nch) for h in range(HEADS)], axis=0)
    gc = per_item(lambda ch, h: gcum[ch][:, HEADS + h:HEADS + h + 1])
    bcol = per_item(lambda ch, h: beta[chunks[ch], h:h + 1])

    q = _items(qkv, nch, 0, hd)
    k = _items(qkv, nch, qk, hd)
    v = _items(qkv, nch, 2 * qk, hd)
    eg = jnp.exp(gc)
    kb = k * bcol
    qg = q * eg
    rhs = jnp.concatenate([v * bcol, kb * eg], axis=-1).astype(BF16)
    gate = _silu(_items(gt, nch, 0, hd))
    rid = _idiv(lax.broadcasted_iota(jnp.int32, (1, CHUNK, 1), 1), lc)

    lane = lax.broadcasted_iota(jnp.int32, (1, 1, 2 * CHUNK), 2)
    col = lane & (CHUNK - 1)
    row = lax.broadcasted_iota(jnp.int32, (1, CHUNK, 1), 1)
    same = _idiv(row, lc) == _idiv(col, lc)
    incl = same & (col <= row)
    strict = same & (col < row)
    gcp = jnp.where(lane < CHUNK, _even(gc), _odd(gc))
    grp = jnp.sum(jnp.where(col == row, gcp, 0.0), axis=1, keepdims=True)
    decay = jnp.where(incl, jnp.exp(jnp.where(incl, gcp - grp, 0.0)), 0.0)
    kst = _stack(k.astype(BF16))
    kk_qk = _bmm_nt(jnp.concatenate([_side(kb.astype(BF16)), _side(q.astype(BF16))], axis=1), kst)
    a = jnp.where(strict, kk_qk[:, :CHUNK] * decay, 0.0)
    attn = jnp.where(incl, kk_qk[:, CHUNK:] * decay, 0.0)
    tinv = _packed_unit_lower_inverse(a, min(16, lc), lc, _idiv(lane, CHUNK), row, col)
    th, tl = _split2(tinv)
    sol = _bmm(jnp.concatenate([th, tl], axis=1), _stack(rhs))
    sol = sol[:, :CHUNK] + sol[:, CHUNK:]
    uv, w = _halves(sol, 0, hd), _halves(sol, hd, hd)

    for ch in range(nch):
        it = slice(ch * HEADS, (ch + 1) * HEADS)
        pr = slice(ch * HEADS // 2, (ch + 1) * HEADS // 2)
        us, os_ = [], []
        for n in range(nsq):
            sl = slice(n * lc, (n + 1) * lc)
            st = s_scr[_state_index(ch, n, nsq, lc, lr)]
            ws = _bmm(jnp.concatenate([w[it, sl], qg[it, sl]], axis=1), st)
            us.append(uv[it, sl] - ws[:, :lc])
            os_.append(ws[:, lc:])
        u = us[0] if nsq == 1 else jnp.concatenate(us, axis=1)
        o = (os_[0] if nsq == 1 else jnp.concatenate(os_, axis=1)) + _halves(_bmm(attn[pr], _stack(u.astype(BF16))), 0, hd)
        for n in range(nsq):
            si = _state_index(ch, n, nsq, lc, lr)
            gl = gc[it, (n + 1) * lc - 1:(n + 1) * lc, :]
            if nsq > 1:
                kd = jnp.where(rid == n, k[it] * jnp.exp(jnp.where(rid == n, gl - gc[it], 0.0)), 0.0)
            else:
                kd = k[it] * jnp.exp(gl - gc[it])
            s_scr[si] = s_scr[si] * jnp.exp(gl) + _bmm(jnp.swapaxes(kd, 1, 2), u)
        _store_heads(og_ref, ch, nsq, lc, _rms(o, ng_ref[...]) * gate[it])

    @pl.when(t == last)
    def _():
        for n in range(ns):
            _write_state(snew_ref, n, s_scr[n])


def _stacked_state(states, j, prev, ns):
    tail = states.shape[2:]
    in_spec = pl.BlockSpec((None, ns) + tail, lambda b, t: (j, b, 0, 0, 0))
    if prev is None:
        out_spec = pl.BlockSpec((states.shape[0], ns) + tail, lambda b, t: (0, b, 0, 0, 0))
        extra_in, extra_specs = [], []
    else:
        out_spec = in_spec
        extra_in, extra_specs = [prev], [pl.BlockSpec(memory_space=pl.ANY)]
    return in_spec, out_spec, extra_in, extra_specs, jax.ShapeDtypeStruct(states.shape, states.dtype)


def _write_state(snew_ref, n, value):
    if len(snew_ref.shape) == 5:
        for m in range(snew_ref.shape[0]):
            snew_ref[m, n] = value
    else:
        snew_ref[n] = value


def _gdn_call(qkv, gt, ba, states, j, prev, alog, dtb, ng, nch, nsq, lc):
    assert 2 * CHUNK == LANES and HEADS % 2 == 0, "two heads' (CHUNK, CHUNK) matrices fill one lane tile"
    bsz, seq, w3 = qkv.shape
    qk = w3 // 3
    hd = HEAD_DIM
    tri = _gdn_consts(nsq, lc)
    ns, lr = (1, nch * CHUNK) if nsq == 1 else (nch * nsq, lc)
    tok = lambda n: pl.BlockSpec((ns, lr, n), lambda b, t: (b, t, 0))
    vec = lambda n: pl.BlockSpec((1, n), lambda b, t: (0, 0))
    state, state_out, extra_in, extra_specs, state_shape = _stacked_state(states, j, prev, ns)
    inputs = [qkv, gt, ba, states, alog, dtb, ng.reshape(1, hd), tri] + extra_in
    return pl.pallas_call(
        functools.partial(_gdn_kernel, nch=nch, nsq=nsq, lc=lc),
        grid=(bsz // ns, seq // lr),
        in_specs=[tok(w3), tok(qk), tok(hd), state, vec(hd), vec(hd), vec(hd),
                  pl.BlockSpec((CHUNK, CHUNK), lambda b, t: (0, 0))] + extra_specs,
        out_specs=[tok(qk), state_out],
        out_shape=[jax.ShapeDtypeStruct((bsz, seq, qk), F32), state_shape],
        input_output_aliases={len(inputs) - 1: 1} if extra_in else {},
        scratch_shapes=[pltpu.VMEM((ns, HEADS, hd, hd), F32)],
        compiler_params=_params("arbitrary", "arbitrary"),
        name="gdn_core",
    )(*inputs)


def _hgrn_kernel(proj_ref, s0_ref, lbraw_ref, ng_ref, cm_ref, *rest, nch, nsq, lc, layer):
    og_ref, snew_ref, st_scr = rest[-3:]
    ns, lr, w4 = proj_ref.shape
    r = ns * lr
    hd = HEAD_DIM
    qf = HEADS * hd
    t = pl.program_id(1)
    last = pl.num_programs(1) - 1

    @pl.when(t == 0)
    def _():
        for n in range(ns):
            st_scr[n] = jnp.swapaxes(s0_ref[n], 1, 2)

    lbraw = lbraw_ref[...]
    e = jnp.exp(lbraw - jnp.max(lbraw, axis=0, keepdims=True))
    p = e / jnp.sum(e, axis=0, keepdims=True)
    lb = jnp.zeros((1, qf), F32)
    for m in range(1, layer + 1):
        lb = lb + p[m:m + 1, :]

    proj = proj_ref[...].reshape(r, w4)
    fr = proj[:, qf:2 * qf]
    efr = jnp.exp(-jnp.abs(fr))
    big = 1.0 / (1.0 + efr)
    small = efr * big
    sig_pos = jnp.where(fr >= 0.0, big, small)
    sig_neg = jnp.where(fr >= 0.0, small, big)
    forget = jnp.maximum(lb, LB_FLOOR) + (1.0 - lb) * sig_pos
    logf = jnp.log(forget)

    levels = _hgrn_levels(lc)
    nmat = 2 + len(levels)
    cm = cm_ref[...]
    ex = jnp.concatenate([jnp.exp(_dot_sel(cm, logf[ch * CHUNK:(ch + 1) * CHUNK], 2)) for ch in range(nch)], axis=0)
    factor = lambda m: jnp.concatenate(
        [ex[(ch * nmat + m) * CHUNK:(ch * nmat + m + 1) * CHUNK, h * hd:(h + 1) * hd][None]
         for ch in range(nch) for h in range(HEADS)], axis=0)

    q = _silu(_items(proj, nch, 0, hd))
    k = _items((1.0 - lb) * sig_neg, nch, 0, hd)
    v = _items(proj, nch, 2 * qf, hd)
    gate = _silu(_items(proj, nch, 3 * qf, hd))

    row = lax.broadcasted_iota(jnp.int32, (CHUNK, CHUNK), 0)
    col = lax.broadcasted_iota(jnp.int32, (CHUNK, CHUNK), 1)
    qb, kb = q.astype(BF16), k.astype(BF16)
    attn = jnp.where((row == col)[None], _bmm_nt(qb, kb), 0.0)
    odd = (lax.broadcasted_iota(jnp.int32, (1, CHUNK, 1), 1) & 1) == 1
    factors = [jnp.where(odd, _items(forget, nch, 0, hd), 1.0)] + [factor(2 + li) for li in range(len(levels))]
    for s, fl in zip([2] + levels, factors):
        mask = (_idiv(row, s) == _idiv(col, s)) & ((row & (s - 1)) >= s // 2) & ((col & (s - 1)) < s // 2)
        flb = fl.astype(BF16)
        attn = jnp.where(mask[None], _bmm_nt(qb * flb, kb * flb), attn)
    eg = factor(0)
    qe = q * eg
    kd = k * factor(1)
    cid = _idiv(lax.broadcasted_iota(jnp.int32, (1, 1, CHUNK), 2), lc)

    for ch in range(nch):
        it = slice(ch * HEADS, (ch + 1) * HEADS)
        vt = jnp.swapaxes(v[it], 1, 2)
        os_ = []
        for n in range(nsq):
            sl = slice(n * lc, (n + 1) * lc)
            si = _state_index(ch, n, nsq, lc, lr)
            st = st_scr[si]
            os_.append(_bmm_nt(qe[it, sl], st))
            vtn = vt if nsq == 1 else jnp.where(cid == n, vt, 0.0)
            egl = eg[it, (n + 1) * lc - 1:(n + 1) * lc, :]
            st_scr[si] = st * egl + _bmm(vtn, kd[it])
        o = (os_[0] if nsq == 1 else jnp.concatenate(os_, axis=1)) + _bmm(attn[it], v[it])
        _store_heads(og_ref, ch, nsq, lc, _rms(o, ng_ref[...]) * gate[it])

    @pl.when(t == last)
    def _():
        for n in range(ns):
            _write_state(snew_ref, n, jnp.swapaxes(st_scr[n], 1, 2))


def _hgrn_call(proj, states, j, prev, lbraw, ng, nch, nsq, lc):
    bsz, seq, w4 = proj.shape
    qf = w4 // 4
    hd = HEAD_DIM
    cm = _hgrn_consts(nsq, lc)
    ns, lr = (1, nch * CHUNK) if nsq == 1 else (nch * nsq, lc)
    tok = lambda n: pl.BlockSpec((ns, lr, n), lambda b, t: (b, t, 0))
    state, state_out, extra_in, extra_specs, state_shape = _stacked_state(states, j, prev, ns)
    inputs = [proj, states, lbraw, ng.reshape(1, hd), cm] + extra_in
    return pl.pallas_call(
        functools.partial(_hgrn_kernel, nch=nch, nsq=nsq, lc=lc, layer=j),
        grid=(bsz // ns, seq // lr),
        in_specs=[tok(w4), state,
                  pl.BlockSpec(lbraw.shape, lambda b, t: (0, 0)),
                  pl.BlockSpec((1, hd), lambda b, t: (0, 0)),
                  pl.BlockSpec(cm.shape, lambda b, t: (0, 0))] + extra_specs,
        out_specs=[tok(qf), state_out],
        out_shape=[jax.ShapeDtypeStruct((bsz, seq, qf), F32), state_shape],
        input_output_aliases={len(inputs) - 1: 1} if extra_in else {},
        scratch_shapes=[pltpu.VMEM((ns, HEADS, hd, hd), F32)],
        compiler_params=_params("arbitrary", "arbitrary"),
        name="hgrn_core",
    )(*inputs)


def _prep_weights(gdn_w_in, gdn_a_log, gdn_dt_bias, hgrn_w_in, gdn_w_out, hgrn_w_out, ffn_w_gu, ffn_w_down):
    pad = HEAD_DIM - 2 * HEADS
    w_gdn = jnp.pad(gdn_w_in, ((0, 0), (0, 0), (0, pad))).astype(BF16)
    gate_pad = lambda a: jnp.pad(a, ((0, 0), (HEADS, HEAD_DIM - 2 * HEADS)))[:, None, :]
    return dict(
        w_gdn=w_gdn, alog=gate_pad(gdn_a_log), dtb=gate_pad(gdn_dt_bias),
        w_hgrn=hgrn_w_in.astype(BF16), gdn_w_out=gdn_w_out.astype(BF16), hgrn_w_out=hgrn_w_out.astype(BF16),
        wgu=ffn_w_gu.astype(BF16), wd=ffn_w_down.astype(BF16))


def _trunk(x, mod, s_gdn, s_gconv, s_hgrn, s_fconv, wts, p, nb, lb, nsq, lc):
    depth = mod.shape[0]
    bsz, seq, _ = x.shape
    qk = HEADS * HEAD_DIM
    if nsq == 1:
        chunks_avail = seq // CHUNK
    else:
        state_bytes = (3 + 2 * max(s_gdn.shape[0], s_hgrn.shape[0])) * nsq * HEADS * HEAD_DIM * HEAD_DIM * 4
        chunks_avail = max(1, min(bsz // nsq, STATE_VMEM_BUDGET // state_bytes))
    gdn_nch = min(GDN_CHUNKS, chunks_avail)
    hgrn_nch = min(HGRN_CHUNKS, chunks_avail)
    new_gdn, new_gconv, new_hgrn, new_fconv = None, [], None, []
    for layer in range(depth):
        modl = mod[layer][:, None, :]
        j = layer // N_MIXERS
        if layer % N_MIXERS == 0:
            qkv, gt, ba, cb_new = _inproj_gdn_call(x, modl, wts["norm_pre_mix"][layer], p["w_gdn"], j,
                                                   wts["gdn_conv_w"][j], wts["gdn_conv_b"][j], s_gconv, nb, lb)
            og, new_gdn = _gdn_call(qkv, gt, ba, s_gdn, j, new_gdn, p["alog"][j], p["dtb"][j],
                                    wts["gdn_norm"][j], gdn_nch, nsq, lc)
            new_gconv.append(cb_new)
            w_out = p["gdn_w_out"]
        else:
            (proj,) = _inproj_call(x, modl, wts["norm_pre_mix"][layer], p["w_hgrn"], j, (4 * qk,), nb, lb)
            og, new_hgrn = _hgrn_call(proj, s_hgrn, j, new_hgrn, wts["hgrn_lb"], wts["hgrn_norm"][j],
                                      hgrn_nch, nsq, lc)
            w_out = p["hgrn_w_out"]
        x, fb_new = _ffn_call(og, x, modl, wts["norm_post_mix"][layer], wts["norm_pre_ffn"][layer],
                              wts["norm_post_ffn"][layer], w_out, j, p["wgu"],
                              wts["ffn_conv_w"][layer], wts["ffn_conv_b"][layer], p["wd"], layer, s_fconv, nb, lb)
        new_fconv.append(fb_new)
    return x, new_gdn, jnp.stack(new_gconv), new_hgrn, jnp.stack(new_fconv)


def kernel(x_prompt, x_sample, state_gdn, state_gdn_conv, state_hgrn, state_ffn_conv, c_prompt, c_sample, ada_w, ada_b, norm_pre_mix, norm_post_mix, norm_pre_ffn, norm_post_ffn, gdn_w_in, gdn_conv_w, gdn_conv_b, gdn_a_log, gdn_dt_bias, gdn_norm, gdn_w_out, hgrn_lb, hgrn_w_in, hgrn_norm, hgrn_w_out, ffn_w_gu, ffn_conv_w, ffn_conv_b, ffn_w_down):
    wts = dict(norm_pre_mix=norm_pre_mix, norm_post_mix=norm_post_mix, norm_pre_ffn=norm_pre_ffn,
               norm_post_ffn=norm_post_ffn, gdn_conv_w=gdn_conv_w, gdn_conv_b=gdn_conv_b, gdn_norm=gdn_norm,
               hgrn_lb=hgrn_lb, hgrn_norm=hgrn_norm, ffn_conv_w=ffn_conv_w, ffn_conv_b=ffn_conv_b)
    p = _prep_weights(gdn_w_in, gdn_a_log, gdn_dt_bias, hgrn_w_in, gdn_w_out, hgrn_w_out, ffn_w_gu, ffn_w_down)

    bp, sp, _ = x_prompt.shape
    bs, ss, _ = x_sample.shape
    mod = _mod_call(jnp.concatenate([c_prompt, c_sample], axis=0), ada_w, ada_b)
    mod_p, mod_s = mod[:, :bp], mod[:, bp:]

    dt = x_prompt.dtype
    zeros = lambda a: jnp.zeros((a.shape[0], bp) + a.shape[2:], dt)
    y_p, p_gdn, p_gconv, p_hgrn, p_fconv = _trunk(
        x_prompt, mod_p, zeros(state_gdn), zeros(state_gdn_conv), zeros(state_hgrn), zeros(state_ffn_conv),
        wts, p, nb=1, lb=min(TILE_ROWS, sp), nsq=1, lc=min(CHUNK, sp))
    y_s, s_gdn, s_gconv, s_hgrn, s_fconv = _trunk(
        x_sample, mod_s, state_gdn, state_gdn_conv, state_hgrn, state_ffn_conv,
        wts, p, nb=min(bs, SHORT_TILE_ROWS // ss), lb=ss, nsq=min(bs, CHUNK // ss), lc=ss)
    return (y_p, y_s, p_gdn, p_gconv, p_hgrn, p_fconv, s_gdn, s_gconv, s_hgrn, s_fconv)
```
